```python
import math
import jax
import jax.numpy as jnp
from jax import lax
import numpy as np

D_MODEL = 4096
BATCH = 8
SEQ = 2048
DEPTH = 1
DEC_BATCH = 8
DEC_SEQ = 64
PAST_LEN = 4096

CHUNK = 64
EPS = 1e-6
GDN_QK_HEADS = 8
GDN_V_HEADS = 16
GDN_HEAD_DIM = 128
GDN_KEY_DIM = GDN_QK_HEADS * GDN_HEAD_DIM
GDN_VAL_DIM = GDN_V_HEADS * GDN_HEAD_DIM
CONV_WIDTH = 4
CONV_DIM = 2 * GDN_KEY_DIM + GDN_VAL_DIM
GLA_HEADS = 4
GLA_KEY_DIM = D_MODEL // 4
GLA_VAL_DIM = D_MODEL // 2
GLA_HEAD_K = GLA_KEY_DIM // GLA_HEADS
GLA_HEAD_V = GLA_VAL_DIM // GLA_HEADS
GLA_GATE_RANK = 16
GLA_GATE_TEMP = 16.0
GLA_LOG_DECAY_FLOOR = -1.0
N_BRANCH = 2
MIX_WIDTH = GDN_VAL_DIM + GLA_VAL_DIM
PROJ_SIZES = (CONV_DIM, GDN_V_HEADS, GDN_V_HEADS, GDN_VAL_DIM,
              GLA_KEY_DIM, GLA_KEY_DIM, GLA_VAL_DIM, GLA_GATE_RANK, GLA_VAL_DIM,
              N_BRANCH * D_MODEL)
PROJ_DIM = sum(PROJ_SIZES)
N_EXPERTS = 32
TOP_K = 4
D_EXPERT = D_MODEL
SWIGLU_LIMIT = 7.0
SWIGLU_ALPHA = 1.702
MOE_BLOCK = 128
N_MOD = 6

kernel_name = 'hybrid_gdn_gla_moe_stream_step'


def rms_norm(x, g):
    xf = x.astype(jnp.float32)
    y = xf * lax.rsqrt(jnp.mean(xf * xf, axis=-1, keepdims=True) + EPS)
    return (y * g.astype(jnp.float32)).astype(x.dtype)


def l2_normalize(x):
    return x * lax.rsqrt(jnp.sum(x * x, axis=-1, keepdims=True) + EPS)


def split_heads(x, n):
    b, l, _ = x.shape
    return x.reshape(b, l, n, -1).transpose(0, 2, 1, 3)


def merge_heads(x):
    b, n, l, d = x.shape
    return x.transpose(0, 2, 1, 3).reshape(b, l, n * d)


def to_chunks(x, chunk):
    b, h, l = x.shape[:3]
    x = x.reshape((b, h, l // chunk, chunk) + x.shape[3:])
    return jnp.moveaxis(x, 2, 0)


def from_chunks(x):
    n, b, h, c = x.shape[:4]
    return jnp.moveaxis(x, 0, 2).reshape((b, h, n * c) + x.shape[4:])


def causal_conv_silu(u, buf, w):
    full = jnp.concatenate([buf, u], axis=1)
    seq = u.shape[1]
    out = full[:, 0:seq] * w[0]
    for i in range(1, CONV_WIDTH):
        out = out + full[:, i:i + seq] * w[i]
    return jax.nn.silu(out), full[:, -(CONV_WIDTH - 1):]


def gated_delta_rule(q, k, v, g, beta, s0, chunk):
    dv = v.shape[-1]
    q, k, v, g, beta = (to_chunks(t, chunk) for t in (q, k, v, g, beta))
    gc = jnp.cumsum(g, axis=-1)
    incl = jnp.tril(jnp.ones((chunk, chunk), dtype=bool))
    strict = jnp.tril(jnp.ones((chunk, chunk), dtype=bool), k=-1)
    decay = jnp.exp(jnp.where(incl, gc[..., :, None] - gc[..., None, :], -jnp.inf))
    k_beta = k * beta[..., None]
    lower = jnp.where(strict, jnp.einsum('nbhid,nbhjd->nbhij', k_beta, k) * decay, 0.0)
    rhs = jnp.concatenate([v * beta[..., None], k_beta * jnp.exp(gc)[..., None]], axis=-1)
    sol = lax.linalg.triangular_solve(lower, rhs, left_side=True, lower=True, unit_diagonal=True)
    u, w = sol[..., :dv], sol[..., dv:]
    attn = jnp.einsum('nbhid,nbhjd->nbhij', q, k) * decay
    q_dec = q * jnp.exp(gc)[..., None]
    k_dec = k * jnp.exp(gc[..., -1:] - gc)[..., None]
    chunk_decay = jnp.exp(gc[..., -1])[..., None, None]

    def step(s, xs):
        u_n, w_n, q_n, k_n, a_n, d_n = xs
        v_new = u_n - jnp.einsum('bhid,bhde->bhie', w_n, s)
        o_n = jnp.einsum('bhid,bhde->bhie', q_n, s) + jnp.einsum('bhij,bhje->bhie', a_n, v_new)
        s = s * d_n + jnp.einsum('bhid,bhie->bhde', k_n, v_new)
        return s, o_n

    s, o = lax.scan(step, s0, (u, w, q_dec, k_dec, attn, chunk_decay))
    return from_chunks(o), s


def gla_recurrence(q, k, v, gk, s0, chunk):
    q, k, v, gk = (to_chunks(t, chunk) for t in (q, k, v, gk))
    b = jnp.cumsum(gk, axis=-2)
    incl = jnp.tril(jnp.ones((chunk, chunk), dtype=bool))
    q_dec = q * jnp.exp(b)
    k_inv = k * jnp.exp(-b)
    k_dec = k * jnp.exp(b[..., -1:, :] - b)
    attn = jnp.where(incl, jnp.einsum('nbhid,nbhjd->nbhij', q_dec, k_inv), 0.0)
    intra = jnp.einsum('nbhij,nbhje->nbhie', attn, v)
    chunk_decay = jnp.exp(b[..., -1, :])[..., None]

    def step(s, xs):
        q_n, k_n, v_n, o_n, d_n = xs
        o_n = o_n + jnp.einsum('bhid,bhde->bhie', q_n, s)
        s = s * d_n + jnp.einsum('bhid,bhie->bhde', k_n, v_n)
        return s, o_n

    s, o = lax.scan(step, s0, (q_dec, k_dec, v, intra, chunk_decay))
    return from_chunks(o), s


def token_mixer(h, conv_buf, s_gdn, s_gla, chunk, w_in, conv_w, a_log, dt_bias, gdn_norm,
                gla_w2, gla_b, gla_norm, w_branch, w_out):
    f32 = jnp.float32
    bsz, seq, _ = h.shape
    splits = [int(s) for s in np.cumsum(PROJ_SIZES)[:-1]]
    qkv, a_raw, b_raw, z, lq, lk, lv, lg, lr, br = jnp.split(h @ w_in, splits, axis=-1)
    qkv, conv_new = causal_conv_silu(qkv, conv_buf, conv_w)
    q, k, v = jnp.split(qkv.astype(f32), [GDN_KEY_DIM, 2 * GDN_KEY_DIM], axis=-1)
    rep = GDN_V_HEADS // GDN_QK_HEADS
    q = jnp.repeat(l2_normalize(split_heads(q, GDN_QK_HEADS)), rep, axis=1) * GDN_HEAD_DIM ** -0.5
    k = jnp.repeat(l2_normalize(split_heads(k, GDN_QK_HEADS)), rep, axis=1)
    v = split_heads(v, GDN_V_HEADS)
    g = -jnp.exp(a_log.astype(f32)) * jax.nn.softplus(a_raw.astype(f32) + dt_bias.astype(f32))
    beta = jax.nn.sigmoid(b_raw.astype(f32))
    o_a, gdn_new = gated_delta_rule(q, k, v, g.transpose(0, 2, 1), beta.transpose(0, 2, 1),
                                    s_gdn.astype(f32), chunk)
    o_a = merge_heads(rms_norm(o_a, gdn_norm)).astype(h.dtype) * jax.nn.silu(z)
    lq = split_heads(lq.astype(f32), GLA_HEADS) * GLA_HEAD_K ** -0.5
    lk = split_heads(lk.astype(f32), GLA_HEADS)
    lv = split_heads(lv.astype(f32), GLA_HEADS)
    gk = jax.nn.log_sigmoid((lg @ gla_w2 + gla_b).astype(f32)) / GLA_GATE_TEMP
    gk = split_heads(jnp.maximum(gk, GLA_LOG_DECAY_FLOOR), GLA_HEADS)
    o_b, gla_new = gla_recurrence(lq, lk, lv, gk, s_gla.astype(f32), chunk)
    o_b = merge_heads(rms_norm(o_b, gla_norm)).astype(h.dtype) * jax.nn.silu(lr)
    gates = jax.nn.sigmoid(br).reshape(bsz, seq, N_BRANCH, D_MODEL)
    y = (gates[..., 0, :] * (o_a @ w_branch[:GDN_VAL_DIM])
         + gates[..., 1, :] * (o_b @ w_branch[GDN_VAL_DIM:]))
    return y @ w_out, conv_new, gdn_new.astype(s_gdn.dtype), gla_new.astype(s_gla.dtype)


def clamped_swiglu(gu):
    x_glu, x_lin = jnp.split(gu, 2, axis=-1)
    x_glu = jnp.minimum(x_glu, SWIGLU_LIMIT)
    x_lin = jnp.clip(x_lin, -SWIGLU_LIMIT, SWIGLU_LIMIT)
    return x_glu * jax.nn.sigmoid(SWIGLU_ALPHA * x_glu) * (x_lin + 1.0)


def moe_ffn(h, w_router, b_router, w_gate_up, b_gate_up, w_down, b_down):
    bsz, seq, d = h.shape
    x = h.reshape(bsz * seq, d)
    n_tok = x.shape[0]
    logits = (x @ w_router + b_router).astype(jnp.float32)
    top_logits, top_idx = lax.top_k(logits, TOP_K)
    top_w = jax.nn.softmax(top_logits, axis=-1)
    n_assign = n_tok * TOP_K
    flat_e = top_idx.reshape(-1)
    order = jnp.argsort(flat_e)
    sorted_e = flat_e[order]
    counts = jnp.zeros((N_EXPERTS,), jnp.int32).at[flat_e].add(1)
    padded = (counts + MOE_BLOCK - 1) // MOE_BLOCK * MOE_BLOCK
    start = jnp.cumsum(counts) - counts
    pad_end = jnp.cumsum(padded)
    pad_start = pad_end - padded
    dest = pad_start[sorted_e] + jnp.arange(n_assign, dtype=jnp.int32) - start[sorted_e]
    n_blocks = -(-n_assign // MOE_BLOCK) + N_EXPERTS
    n_rows = n_blocks * MOE_BLOCK
    row_token = jnp.full((n_rows,), n_tok, jnp.int32).at[dest].set((order // TOP_K).astype(jnp.int32))
    row_gate = jnp.zeros((n_rows,), jnp.float32).at[dest].set(top_w.reshape(-1)[order])
    block_start = jnp.arange(n_blocks, dtype=jnp.int32) * MOE_BLOCK
    block_expert = jnp.minimum(jnp.searchsorted(pad_end, block_start, side='right'), N_EXPERTS - 1)
    x_pad = jnp.concatenate([x, jnp.zeros((1, d), x.dtype)], axis=0)

    def expert_block(xs):
        rows, gates, e = xs
        gu = x_pad[rows] @ w_gate_up[e] + b_gate_up[e]
        y = clamped_swiglu(gu) @ w_down[e] + b_down[e]
        return y * gates[:, None].astype(y.dtype)

    yb = lax.map(expert_block, (row_token.reshape(n_blocks, MOE_BLOCK),
                                row_gate.reshape(n_blocks, MOE_BLOCK), block_expert))
    y = jax.ops.segment_sum(yb.reshape(n_rows, d), row_token, num_segments=n_tok + 1)[:n_tok]
    return y.reshape(bsz, seq, d)


def layer(x, c, conv_buf, s_gdn, s_gla, chunk, lw):
    (w_ada, b_ada, n_mix_pre, n_mix_post, n_ffn_pre, n_ffn_post, w_in, conv_w, a_log, dt_bias,
     gdn_norm, gla_w2, gla_b, gla_norm, w_branch, w_out, w_router, b_router, w_gate_up, b_gate_up,
     w_down, b_down) = lw
    mod = jax.nn.silu(c) @ w_ada + b_ada
    shift1, scale1, gate1, shift2, scale2, gate2 = jnp.split(mod[:, None, :], N_MOD, axis=-1)
    h = rms_norm(x, n_mix_pre) * (1.0 + scale1) + shift1
    y, conv_new, gdn_new, gla_new = token_mixer(h, conv_buf, s_gdn, s_gla, chunk, w_in, conv_w, a_log,
                                                dt_bias, gdn_norm, gla_w2, gla_b, gla_norm, w_branch, w_out)
    x = x + gate1 * rms_norm(y, n_mix_post)
    h = rms_norm(x, n_ffn_pre) * (1.0 + scale2) + shift2
    y = moe_ffn(h, w_router, b_router, w_gate_up, b_gate_up, w_down, b_down)
    x = x + gate2 * rms_norm(y, n_ffn_post)
    return x, conv_new, gdn_new, gla_new


def setup_inputs(seed: int = 0) -> dict:
    key = jax.random.key(seed)
    ks = list(jax.random.split(key, 32))
    f32 = jnp.float32

    def nrm(i, shape, scale):
        return scale * jax.random.normal(ks[i], shape, f32)

    dt = jnp.exp(jax.random.uniform(ks[16], (DEPTH, GDN_V_HEADS), f32,
                                    minval=math.log(1e-3), maxval=math.log(1e-1)))
    return {
        'x_prompt': nrm(0, (BATCH, SEQ, D_MODEL), 1.0),
        'x_sample': nrm(1, (DEC_BATCH, DEC_SEQ, D_MODEL), 1.0),
        'c_prompt': nrm(2, (BATCH, D_MODEL), 1.0),
        'c_sample': nrm(3, (DEC_BATCH, D_MODEL), 1.0),
        'state_conv': nrm(4, (DEPTH, DEC_BATCH, CONV_WIDTH - 1, CONV_DIM), 1.0),
        'state_gdn': nrm(5, (DEPTH, DEC_BATCH, GDN_V_HEADS, GDN_HEAD_DIM, GDN_HEAD_DIM), 1.0),
        'state_gla': nrm(6, (DEPTH, DEC_BATCH, GLA_HEADS, GLA_HEAD_K, GLA_HEAD_V), 1.0),
        'w_ada': nrm(7, (DEPTH, D_MODEL, N_MOD * D_MODEL), 0.5 * D_MODEL ** -0.5),
        'b_ada': nrm(8, (DEPTH, N_MOD * D_MODEL), 0.02),
        'norm_mix_pre': 1.0 + nrm(9, (DEPTH, D_MODEL), 0.01),
        'norm_mix_post': 1.0 + nrm(10, (DEPTH, D_MODEL), 0.01),
        'norm_ffn_pre': 1.0 + nrm(11, (DEPTH, D_MODEL), 0.01),
        'norm_ffn_post': 1.0 + nrm(12, (DEPTH, D_MODEL), 0.01),
        'w_in': nrm(13, (DEPTH, D_MODEL, PROJ_DIM), D_MODEL ** -0.5),
        'conv_w': nrm(14, (DEPTH, CONV_WIDTH, CONV_DIM), CONV_WIDTH ** -0.5),
        'gdn_a_log': jnp.log(jax.random.uniform(ks[15], (DEPTH, GDN_V_HEADS), f32, minval=1.0, maxval=16.0)),
        'gdn_dt_bias': dt + jnp.log(-jnp.expm1(-dt)),
        'gdn_norm': 1.0 + nrm(17, (DEPTH, GDN_HEAD_DIM), 0.01),
        'gla_gate_w2': nrm(18, (DEPTH, GLA_GATE_RANK, GLA_KEY_DIM), GLA_GATE_RANK ** -0.5),
        'gla_gate_b': nrm(19, (DEPTH, GLA_KEY_DIM), 0.1),
        'gla_norm': 1.0 + nrm(20, (DEPTH, GLA_HEAD_V), 0.01),
        'w_branch': nrm(21, (DEPTH, MIX_WIDTH, D_MODEL), (MIX_WIDTH // N_BRANCH) ** -0.5),
        'w_out': nrm(22, (DEPTH, D_MODEL, D_MODEL), D_MODEL ** -0.5),
        'w_router': nrm(23, (DEPTH, D_MODEL, N_EXPERTS), D_MODEL ** -0.5),
        'b_router': nrm(24, (DEPTH, N_EXPERTS), 0.01),
        'w_gate_up': nrm(25, (DEPTH, N_EXPERTS, D_MODEL, 2 * D_EXPERT), D_MODEL ** -0.5),
        'b_gate_up': nrm(26, (DEPTH, N_EXPERTS, 2 * D_EXPERT), 0.01),
        'w_down': nrm(27, (DEPTH, N_EXPERTS, D_EXPERT, D_MODEL), D_EXPERT ** -0.5),
        'b_down': nrm(28, (DEPTH, N_EXPERTS, D_MODEL), 0.01),
    }


def reference(x_prompt, x_sample, c_prompt, c_sample, state_conv, state_gdn, state_gla,
              w_ada, b_ada, norm_mix_pre, norm_mix_post, norm_ffn_pre, norm_ffn_post,
              w_in, conv_w, gdn_a_log, gdn_dt_bias, gdn_norm, gla_gate_w2, gla_gate_b, gla_norm,
              w_branch, w_out, w_router, b_router, w_gate_up, b_gate_up, w_down, b_down):
    weights = (w_ada, b_ada, norm_mix_pre, norm_mix_post, norm_ffn_pre, norm_ffn_post,
               w_in, conv_w, gdn_a_log, gdn_dt_bias, gdn_norm, gla_gate_w2, gla_gate_b, gla_norm,
               w_branch, w_out, w_router, b_router, w_gate_up, b_gate_up, w_down, b_down)
    bsz = x_prompt.shape[0]
    y_prompt, y_sample = x_prompt, x_sample
    conv_p, gdn_p, gla_p, conv_s, gdn_s, gla_s = [], [], [], [], [], []
    for l in range(DEPTH):
        lw = tuple(w[l] for w in weights)
        conv0 = jnp.zeros((bsz,) + state_conv.shape[2:], state_conv.dtype)
        gdn0 = jnp.zeros((bsz,) + state_gdn.shape[2:], state_gdn.dtype)
        gla0 = jnp.zeros((bsz,) + state_gla.shape[2:], state_gla.dtype)
        y_prompt, cp, gp, lp = layer(y_prompt, c_prompt, conv0, gdn0, gla0, CHUNK, lw)
        y_sample, cs, gs, ls = layer(y_sample, c_sample, state_conv[l], state_gdn[l], state_gla[l],
                                     x_sample.shape[1], lw)
        conv_p.append(cp); gdn_p.append(gp); gla_p.append(lp)
        conv_s.append(cs); gdn_s.append(gs); gla_s.append(ls)
    return (y_prompt, y_sample, jnp.stack(conv_p), jnp.stack(gdn_p), jnp.stack(gla_p),
            jnp.stack(conv_s), jnp.stack(gdn_s), jnp.stack(gla_s))
```

```python
import functools

import jax
import jax.numpy as jnp
from jax import lax
from jax.experimental import pallas as pl
from jax.experimental.pallas import tpu as pltpu

F32 = jnp.float32
BF16 = jnp.bfloat16
I32 = jnp.int32
HIGHEST = lax.Precision.HIGHEST

CHUNK = 64
EPS = 1e-6
CONV_WIDTH = 4
TOP_K = 4
N_MOD = 6
GLA_GATE_TEMP = 16.0
GLA_LOG_DECAY_FLOOR = -1.0
SWIGLU_LIMIT = 7.0
SWIGLU_ALPHA = 1.702
LANES = 128
SUBLANES = 8
NEG_BIG = -1e30
VMEM_LIMIT = 56 * 1024 * 1024


def _cparams(sem):
    return pltpu.CompilerParams(dimension_semantics=sem, vmem_limit_bytes=VMEM_LIMIT)


def _tile(n, target, quantum):
    if n <= target:
        return n
    t = target - target % quantum
    while t > quantum and n % t:
        t -= quantum
    assert n % t == 0, (n, target, quantum)
    return t


def _dot(a, b, precision=None):
    return jnp.dot(a, b, preferred_element_type=F32, precision=precision)


def _dot_nt(a, b, precision=None):
    return lax.dot_general(a, b, (((1,), (1,)), ((), ())), preferred_element_type=F32,
                           precision=precision)


def _silu(x):
    return x * jax.nn.sigmoid(x)


def _softplus(x):
    return jnp.maximum(x, 0.0) + jnp.log1p(jnp.exp(-jnp.abs(x)))


def _rms(x, g):
    return x * lax.rsqrt(jnp.mean(x * x, axis=-1, keepdims=True) + EPS) * g


def _ada_kernel(c_ref, w_ref, b_ref, o_ref):
    a = _silu(c_ref[...]).astype(BF16)
    o_ref[...] = _dot(a, w_ref[...].astype(BF16)) + b_ref[...]


def _ada_proj(c, w, b):
    m, k = c.shape
    n = w.shape[1]
    tn = _tile(n, 512, LANES)
    return pl.pallas_call(
        _ada_kernel, grid=(n // tn,),
        in_specs=[pl.BlockSpec((m, k), lambda j: (0, 0)),
                  pl.BlockSpec((k, tn), lambda j: (0, j)),
                  pl.BlockSpec((1, tn), lambda j: (0, j))],
        out_specs=pl.BlockSpec((m, tn), lambda j: (0, j)),
        out_shape=jax.ShapeDtypeStruct((m, n), F32),
        compiler_params=_cparams(("arbitrary",)), name="ada_proj")(c, w, b.reshape(1, n))


def _mm_kernel(a_ref, w_ref, o_ref):
    o_ref[...] = _dot(a_ref[...], w_ref[...]).astype(o_ref.dtype)


def _matmul(a, w, tm, tn, out_dtype, name):
    m, k = a.shape
    n = w.shape[1]
    return pl.pallas_call(
        _mm_kernel, grid=(n // tn, m // tm),
        in_specs=[pl.BlockSpec((tm, k), lambda j, i: (i, 0)),
                  pl.BlockSpec((k, tn), lambda j, i: (0, j))],
        out_specs=pl.BlockSpec((tm, tn), lambda j, i: (i, j)),
        out_shape=jax.ShapeDtypeStruct((m, n), out_dtype),
        compiler_params=_cparams(("arbitrary", "arbitrary")), name=name)(a, w)


def _xp_spec(d, nch):
    return pl.BlockSpec((None, CHUNK, d), lambda s, b: (b, jnp.minimum(s, nch - 1), 0))


def _xs_spec(d, nch):
    return pl.BlockSpec((None, CHUNK, d), lambda s, b: (jnp.where(s < nch, 0, b), 0, 0))


def _mod_spec(d, nch):
    return pl.BlockSpec((None, None, N_MOD, d), lambda s, b: (jnp.where(s < nch, 0, 1), b, 0, 0))


def _row_spec(width, nb, col=0):
    return pl.BlockSpec((CHUNK, width), lambda s, b: (s * nb + b, col))


def _prenorm_kernel(xp_ref, xs_ref, mod_ref, g_ref, h_ref, *, nch):
    s = pl.program_id(0)
    x = jnp.where(s < nch, xp_ref[...], xs_ref[...])
    shift, scale = mod_ref[0:1, :], mod_ref[1:2, :]
    h_ref[...] = (_rms(x, g_ref[...]) * (1.0 + scale) + shift).astype(h_ref.dtype)


def _prenorm(xp, xs, mod, g, nch):
    nb, _, d = xp.shape
    t = (nch + 1) * nb * CHUNK
    return pl.pallas_call(
        functools.partial(_prenorm_kernel, nch=nch), grid=(nch + 1, nb),
        in_specs=[_xp_spec(d, nch), _xs_spec(d, nch), _mod_spec(d, nch),
                  pl.BlockSpec((1, d), lambda s, b: (0, 0))],
        out_specs=_row_spec(d, nb),
        out_shape=jax.ShapeDtypeStruct((t, d), BF16),
        compiler_params=_cparams(("arbitrary", "arbitrary")), name="prenorm")(xp, xs, mod, g)


def _midnorm_kernel(xp_ref, xs_ref, y_ref, mod_ref, g1_ref, g2_ref, x1_ref, h2_ref, *, nch):
    s = pl.program_id(0)
    x = jnp.where(s < nch, xp_ref[...], xs_ref[...])
    gate1, shift2, scale2 = mod_ref[2:3, :], mod_ref[3:4, :], mod_ref[4:5, :]
    x1 = x + gate1 * _rms(y_ref[...], g1_ref[...])
    x1_ref[...] = x1
    h2_ref[...] = _rms(x1, g2_ref[...]) * (1.0 + scale2) + shift2


def _midnorm(xp, xs, y, mod, g1, g2, nch):
    nb, _, d = xp.shape
    t = (nch + 1) * nb * CHUNK
    vec = pl.BlockSpec((1, d), lambda s, b: (0, 0))
    return pl.pallas_call(
        functools.partial(_midnorm_kernel, nch=nch), grid=(nch + 1, nb),
        in_specs=[_xp_spec(d, nch), _xs_spec(d, nch), _row_spec(d, nb), _mod_spec(d, nch), vec, vec],
        out_specs=[_row_spec(d, nb), _row_spec(d, nb)],
        out_shape=[jax.ShapeDtypeStruct((t, d), F32), jax.ShapeDtypeStruct((t, d), F32)],
        compiler_params=_cparams(("arbitrary", "arbitrary")), name="midnorm")(xp, xs, y, mod, g1, g2)


def _prompt_state_spec(block, nch, nb):
    tail = (0,) * (len(block) - 1)
    return pl.BlockSpec(block, lambda s, b: (jnp.where(s == nch - 1, b, jnp.where(s < nch - 1, 0, nb - 1)),) + tail)


def _sample_state_spec(block, nch):
    tail = (0,) * (len(block) - 1)
    return pl.BlockSpec(block, lambda s, b: (jnp.where(s == nch, b, 0),) + tail)


def _tri(n, strict=False, dtype=F32):
    i = lax.broadcasted_iota(I32, (n, n), 0)
    j = lax.broadcasted_iota(I32, (n, n), 1)
    return ((i > j) if strict else (i >= j)).astype(dtype)


def _gdn_kernel(qkv_ref, z_ref, ab_ref, convw_ref, conv0_ref, alog_ref, dtb_ref, gn_ref, s0_ref,
                oa_ref, convp_ref, convs_ref, gdnp_ref, gdns_ref,
                tails, state, full, act, qn, kn, *, nch, hq, hv, dh, group):
    s = pl.program_id(0)
    b = pl.program_id(1)
    kd = hq * dh
    rep = hv // hq

    @pl.when(s == 0)
    def _():
        tails[b] = jnp.zeros(tails.shape[1:], F32)
        state[b] = jnp.zeros(state.shape[1:], F32)

    @pl.when(s == nch)
    def _():
        tails[b] = conv0_ref[...]
        state[b] = s0_ref[...]

    full[0:SUBLANES, :] = tails[b]
    full[SUBLANES:SUBLANES + CHUNK, :] = qkv_ref[...]
    base = SUBLANES - (CONV_WIDTH - 1)
    acc = full[base:base + CHUNK, :] * convw_ref[0:1, :]
    for i in range(1, CONV_WIDTH):
        acc = acc + full[base + i:base + i + CHUNK, :] * convw_ref[i:i + 1, :]
    act[...] = _silu(acc)
    new_tail = full[CHUNK:CHUNK + SUBLANES, :]
    tails[b] = new_tail

    for h in range(hq):
        q = act[:, h * dh:(h + 1) * dh]
        k = act[:, kd + h * dh:kd + (h + 1) * dh]
        qn[:, h * dh:(h + 1) * dh] = q * lax.rsqrt(jnp.sum(q * q, axis=-1, keepdims=True) + EPS) * dh ** -0.5
        kn[:, h * dh:(h + 1) * dh] = k * lax.rsqrt(jnp.sum(k * k, axis=-1, keepdims=True) + EPS)

    ab = ab_ref[...]
    g_all = -jnp.exp(alog_ref[...]) * _softplus(ab + dtb_ref[...])
    beta_all = jax.nn.sigmoid(ab)
    gc_all = _dot(_tri(CHUNK), g_all, HIGHEST)
    gc_t = gc_all.T
    eg_all = jnp.exp(gc_all)
    g_last = gc_all[CHUNK - 1:CHUNK, :]
    ekd_all = jnp.exp(g_last - gc_all)
    edec_all = jnp.exp(g_last)

    ii = lax.broadcasted_iota(I32, (CHUNK, CHUNK), 0)
    jj = lax.broadcasted_iota(I32, (CHUNK, CHUNK), 1)
    incl = ii >= jj
    strict = ii > jj
    eye = (ii == jj).astype(F32)

    for h0 in range(0, hv, group):
        heads = range(h0, min(h0 + group, hv))
        kh, qh, kb, dec, nn, tm = {}, {}, {}, {}, {}, {}
        for h in heads:
            kh[h] = kn[:, (h // rep) * dh:(h // rep + 1) * dh]
            qh[h] = qn[:, (h // rep) * dh:(h // rep + 1) * dh]
            beta = beta_all[:, hv + h:hv + h + 1]
            kb[h] = kh[h] * beta
            diff = gc_all[:, h:h + 1] - gc_t[h:h + 1, :]
            dec[h] = jnp.exp(jnp.where(incl, diff, NEG_BIG))
        for h in heads:
            a = _dot_nt(kb[h], kh[h], HIGHEST)
            nn[h] = jnp.where(strict, -(a * dec[h]), 0.0)
            tm[h] = eye + nn[h]
        p = CHUNK
        while p > 2:
            for h in heads:
                nn[h] = _dot(nn[h], nn[h], HIGHEST)
            for h in heads:
                tm[h] = tm[h] + _dot(tm[h], nn[h], HIGHEST)
            p //= 2
        u, w, attn = {}, {}, {}
        for h in heads:
            beta = beta_all[:, hv + h:hv + h + 1]
            vh = act[:, 2 * kd + h * dh:2 * kd + (h + 1) * dh]
            rhs = jnp.concatenate([vh * beta, kb[h] * eg_all[:, h:h + 1]], axis=-1)
            sol = _dot(tm[h], rhs, HIGHEST)
            u[h], w[h] = sol[:, :dh], sol[:, dh:]
            attn[h] = _dot_nt(qh[h], kh[h], HIGHEST) * dec[h]
        vnew, o = {}, {}
        for h in heads:
            vnew[h] = u[h] - _dot(w[h], state[b, h], HIGHEST)
        for h in heads:
            o[h] = (_dot(qh[h] * eg_all[:, h:h + 1], state[b, h], HIGHEST)
                    + _dot(attn[h], vnew[h], HIGHEST))
        for h in heads:
            kdec_t = (kh[h] * ekd_all[:, h:h + 1]).T
            state[b, h] = state[b, h] * edec_all[0:1, h:h + 1] + _dot(kdec_t, vnew[h], HIGHEST)
        for h in heads:
            zh = z_ref[:, h * dh:(h + 1) * dh]
            oa_ref[:, h * dh:(h + 1) * dh] = (_rms(o[h], gn_ref[...]) * _silu(zh)).astype(oa_ref.dtype)

    @pl.when(s == nch - 1)
    def _():
        convp_ref[...] = new_tail
        gdnp_ref[...] = state[b]

    @pl.when(s == nch)
    def _():
        convs_ref[...] = new_tail
        gdns_ref[...] = state[b]


def _gdn(p_main, p_small, conv_w, conv0, a_log, dt_bias, gdn_norm, s0, *, nch, nb, hq, hv, dh, z_col):
    t = p_main.shape[0]
    cd = conv_w.shape[1]
    vd = hv * dh
    kd = hq * dh
    vec = lambda n: pl.BlockSpec((1, n), lambda s, b: (0, 0))
    st_block = (None, hv, dh, dh)
    cv_block = (None, SUBLANES, cd)
    return pl.pallas_call(
        functools.partial(_gdn_kernel, nch=nch, hq=hq, hv=hv, dh=dh, group=4), grid=(nch + 1, nb),
        in_specs=[_row_spec(cd, nb), _row_spec(vd, nb, z_col), _row_spec(LANES, nb),
                  pl.BlockSpec((CONV_WIDTH, cd), lambda s, b: (0, 0)),
                  _sample_state_spec(cv_block, nch), vec(LANES), vec(LANES), vec(dh),
                  _sample_state_spec(st_block, nch)],
        out_specs=[_row_spec(vd, nb),
                   _prompt_state_spec(cv_block, nch, nb), _sample_state_spec(cv_block, nch),
                   _prompt_state_spec(st_block, nch, nb), _sample_state_spec(st_block, nch)],
        out_shape=[jax.ShapeDtypeStruct((t, vd), BF16),
                   jax.ShapeDtypeStruct((nb, SUBLANES, cd), F32), jax.ShapeDtypeStruct((nb, SUBLANES, cd), F32),
                   jax.ShapeDtypeStruct((nb, hv, dh, dh), F32), jax.ShapeDtypeStruct((nb, hv, dh, dh), F32)],
        scratch_shapes=[pltpu.VMEM((nb, SUBLANES, cd), F32), pltpu.VMEM((nb, hv, dh, dh), F32),
                        pltpu.VMEM((SUBLANES + CHUNK, cd), F32), pltpu.VMEM((CHUNK, cd), F32),
                        pltpu.VMEM((CHUNK, kd), F32), pltpu.VMEM((CHUNK, kd), F32)],
        compiler_params=_cparams(("arbitrary", "arbitrary")), name="gdn")(
            p_main, p_main, p_small, conv_w, conv0, a_log, dt_bias, gdn_norm, s0)


def _gla_kernel(q_ref, k_ref, v_ref, r_ref, ab_ref, w2_ref, gb_ref, gn_ref, s0_ref,
                ob_ref, glap_ref, glas_ref, state, *, nch, nh, hk, hvd):
    s = pl.program_id(0)
    b = pl.program_id(1)

    @pl.when(s == 0)
    def _():
        state[b] = jnp.zeros(state.shape[1:], F32)

    @pl.when(s == nch)
    def _():
        state[b] = s0_ref[...]

    pre = _dot(ab_ref[...], w2_ref[...], HIGHEST) + gb_ref[...]
    log_sig = jnp.minimum(pre, 0.0) - jnp.log1p(jnp.exp(-jnp.abs(pre)))
    gk = jnp.maximum(log_sig / GLA_GATE_TEMP, GLA_LOG_DECAY_FLOOR)
    bc = _dot(_tri(CHUNK), gk, HIGHEST)
    b_last = bc[CHUNK - 1:CHUNK, :]
    qd = q_ref[...] * hk ** -0.5 * jnp.exp(bc)
    ki = k_ref[...] * jnp.exp(-bc)
    kdec = k_ref[...] * jnp.exp(b_last - bc)
    edec = jnp.exp(b_last)
    incl = _tri(CHUNK) > 0.0

    for h in range(nh):
        ks = slice(h * hk, (h + 1) * hk)
        vs = slice(h * hvd, (h + 1) * hvd)
        vh = v_ref[:, vs]
        attn = jnp.where(incl, _dot_nt(qd[:, ks], ki[:, ks], HIGHEST), 0.0)
        o = _dot(attn, vh, HIGHEST) + _dot(qd[:, ks], state[b, h], HIGHEST)
        edec_col = jnp.broadcast_to(edec[:, ks], (SUBLANES, hk)).T[:, 0:1]
        state[b, h] = state[b, h] * edec_col + _dot(kdec[:, ks].T, vh, HIGHEST)
        ob_ref[:, vs] = (_rms(o, gn_ref[...]) * _silu(r_ref[:, vs])).astype(ob_ref.dtype)

    @pl.when(s == nch - 1)
    def _():
        glap_ref[...] = state[b]

    @pl.when(s == nch)
    def _():
        glas_ref[...] = state[b]


def _gla(p_main, p_small, w2_pad, gate_b, gla_norm, s0, *, nch, nb, nh, hk, hvd, cols):
    t = p_main.shape[0]
    kl, vl = nh * hk, nh * hvd
    st_block = (None, nh, hk, hvd)
    q_col, k_col, v_col, r_col = cols
    return pl.pallas_call(
        functools.partial(_gla_kernel, nch=nch, nh=nh, hk=hk, hvd=hvd), grid=(nch + 1, nb),
        in_specs=[_row_spec(kl, nb, q_col), _row_spec(kl, nb, k_col), _row_spec(vl, nb, v_col),
                  _row_spec(vl, nb, r_col), _row_spec(LANES, nb),
                  pl.BlockSpec((LANES, kl), lambda s, b: (0, 0)),
                  pl.BlockSpec((1, kl), lambda s, b: (0, 0)),
                  pl.BlockSpec((1, hvd), lambda s, b: (0, 0)),
                  _sample_state_spec(st_block, nch)],
        out_specs=[_row_spec(vl, nb), _prompt_state_spec(st_block, nch, nb), _sample_state_spec(st_block, nch)],
        out_shape=[jax.ShapeDtypeStruct((t, vl), BF16),
                   jax.ShapeDtypeStruct((nb, nh, hk, hvd), F32), jax.ShapeDtypeStruct((nb, nh, hk, hvd), F32)],
        scratch_shapes=[pltpu.VMEM((nb, nh, hk, hvd), F32)],
        compiler_params=_cparams(("arbitrary", "arbitrary")), name="gla")(
            p_main, p_main, p_main, p_main, p_small, w2_pad, gate_b, gla_norm, s0)


def _merge_kernel(oa_ref, ob_ref, wa_ref, wb_ref, g0_ref, g1_ref, o_ref):
    ya = _dot(oa_ref[...], wa_ref[...])
    yb = _dot(ob_ref[...], wb_ref[...])
    o_ref[...] = (jax.nn.sigmoid(g0_ref[...]) * ya + jax.nn.sigmoid(g1_ref[...]) * yb).astype(o_ref.dtype)


def _merge(oa, ob, wa, wb, p_main, br_off, tm, tn):
    t, ka = oa.shape
    kb = ob.shape[1]
    d = wa.shape[1]
    c0, c1 = br_off // tn, (br_off + d) // tn
    return pl.pallas_call(
        _merge_kernel, grid=(d // tn, t // tm),
        in_specs=[pl.BlockSpec((tm, ka), lambda j, i: (i, 0)), pl.BlockSpec((tm, kb), lambda j, i: (i, 0)),
                  pl.BlockSpec((ka, tn), lambda j, i: (0, j)), pl.BlockSpec((kb, tn), lambda j, i: (0, j)),
                  pl.BlockSpec((tm, tn), lambda j, i: (i, c0 + j)), pl.BlockSpec((tm, tn), lambda j, i: (i, c1 + j))],
        out_specs=pl.BlockSpec((tm, tn), lambda j, i: (i, j)),
        out_shape=jax.ShapeDtypeStruct((t, d), BF16),
        compiler_params=_cparams(("arbitrary", "arbitrary")), name="merge")(oa, ob, wa, wb, p_main, p_main)


def _router_kernel(h_ref, w_ref, b_ref, idx_ref, wgt_ref, rank_ref, cnt_ref, run, *, tm):
    @pl.when(pl.program_id(0) == 0)
    def _():
        run[...] = jnp.zeros(run.shape, F32)

    logits = _dot(h_ref[...], w_ref[...], HIGHEST) + b_ref[...]
    lane = lax.broadcasted_iota(I32, logits.shape, 1).astype(F32)
    cur = logits
    tops, sels, idxs = [], [], []
    for _ in range(TOP_K):
        m = jnp.max(cur, axis=-1, keepdims=True)
        idx = jnp.min(jnp.where(cur == m, lane, float(LANES)), axis=-1, keepdims=True)
        sel = lane == idx
        tops.append(m)
        sels.append(sel)
        idxs.append(idx)
        cur = jnp.where(sel, 2.0 * NEG_BIG, cur)
    exps = [jnp.exp(m - tops[0]) for m in tops]
    denom = exps[0]
    for e in exps[1:]:
        denom = denom + e
    onehot = sels[0].astype(F32)
    for sel in sels[1:]:
        onehot = onehot + sel.astype(F32)
    prefix = _dot(_tri(tm, strict=True, dtype=BF16), onehot.astype(BF16)) + run[0:1, :]
    idx_out = jnp.zeros(logits.shape, F32)
    wgt_out = jnp.zeros(logits.shape, F32)
    rank_out = jnp.zeros(logits.shape, F32)
    for k in range(TOP_K):
        rank = jnp.sum(jnp.where(sels[k], prefix, 0.0), axis=-1, keepdims=True)
        idx_out = jnp.where(lane == float(k), idxs[k], idx_out)
        wgt_out = jnp.where(lane == float(k), exps[k] / denom, wgt_out)
        rank_out = jnp.where(lane == float(k), rank, rank_out)
    idx_ref[...] = idx_out.astype(I32)
    wgt_ref[...] = wgt_out
    rank_ref[...] = rank_out.astype(I32)
    total = run[...] + jnp.sum(onehot, axis=0, keepdims=True)
    run[...] = total
    cnt_ref[...] = total.astype(I32)


def _router(h2, w_pad, b_pad, tm):
    t, d = h2.shape
    row = pl.BlockSpec((tm, LANES), lambda i: (i, 0))
    return pl.pallas_call(
        functools.partial(_router_kernel, tm=tm), grid=(t // tm,),
        in_specs=[pl.BlockSpec((tm, d), lambda i: (i, 0)), pl.BlockSpec((d, LANES), lambda i: (0, 0)),
                  pl.BlockSpec((1, LANES), lambda i: (0, 0))],
        out_specs=[row, row, row, pl.BlockSpec((SUBLANES, LANES), lambda i: (0, 0))],
        out_shape=[jax.ShapeDtypeStruct((t, LANES), I32), jax.ShapeDtypeStruct((t, LANES), F32),
                   jax.ShapeDtypeStruct((t, LANES), I32), jax.ShapeDtypeStruct((SUBLANES, LANES), I32)],
        scratch_shapes=[pltpu.VMEM((SUBLANES, LANES), F32)],
        compiler_params=_cparams(("arbitrary",)), name="router")(h2, w_pad, b_pad)


def _row_copy(src_hbm, dst, row, slot, sem):
    return pltpu.make_async_copy(src_hbm.at[pl.ds(row, 1)], dst.at[pl.ds(slot, 1)], sem)


def _dispatch_kernel(tok_ref, src_hbm, o_ref, buf, sem, *, tg):
    def issue(r, c):
        _row_copy(src_hbm, buf, tok_ref[r], r, sem).start()
        return c

    def drain(r, c):
        _row_copy(src_hbm, buf, tok_ref[r], r, sem).wait()
        return c

    lax.fori_loop(0, tg, issue, 0)
    lax.fori_loop(0, tg, drain, 0)
    o_ref[...] = buf[...].astype(o_ref.dtype)


def _dispatch(row_token, h2, tg):
    r = row_token.shape[0]
    d = h2.shape[1]
    return pl.pallas_call(
        functools.partial(_dispatch_kernel, tg=tg), grid=(r // tg,),
        in_specs=[pl.BlockSpec((tg,), lambda i: (i,), memory_space=pltpu.SMEM),
                  pl.BlockSpec(memory_space=pl.ANY)],
        out_specs=pl.BlockSpec((tg, d), lambda i: (i, 0)),
        out_shape=jax.ShapeDtypeStruct((r, d), BF16),
        scratch_shapes=[pltpu.VMEM((tg, d), F32), pltpu.SemaphoreType.DMA(())],
        compiler_params=_cparams(("arbitrary",)), name="moe_dispatch")(row_token, h2)


def _expert_changed(be_ref, i):
    return jnp.logical_or(i == 0, be_ref[i] != be_ref[jnp.maximum(i - 1, 0)])


def _gate_up_kernel(be_ref, nu_ref, x_ref, wg_ref, wl_ref, bg_ref, bl_ref, o_ref, wg_bf, wl_bf):
    i = pl.program_id(1)

    @pl.when(i < nu_ref[0])
    def _():
        @pl.when(_expert_changed(be_ref, i))
        def _():
            wg_bf[...] = wg_ref[...].astype(BF16)
            wl_bf[...] = wl_ref[...].astype(BF16)

        x = x_ref[...]
        glu = jnp.minimum(_dot(x, wg_bf[...]) + bg_ref[...], SWIGLU_LIMIT)
        lin = jnp.clip(_dot(x, wl_bf[...]) + bl_ref[...], -SWIGLU_LIMIT, SWIGLU_LIMIT)
        o_ref[...] = (glu * jax.nn.sigmoid(SWIGLU_ALPHA * glu) * (lin + 1.0)).astype(o_ref.dtype)

    @pl.when(i >= nu_ref[0])
    def _():
        o_ref[...] = jnp.zeros(o_ref.shape, o_ref.dtype)


def _down_kernel(be_ref, nu_ref, a_ref, w_ref, b_ref, o_ref, w_bf):
    i = pl.program_id(1)

    @pl.when(i < nu_ref[0])
    def _():
        @pl.when(_expert_changed(be_ref, i))
        def _():
            w_bf[...] = w_ref[...].astype(BF16)

        o_ref[...] = _dot(a_ref[...], w_bf[...]) + b_ref[...]

    @pl.when(i >= nu_ref[0])
    def _():
        o_ref[...] = jnp.zeros(o_ref.shape, o_ref.dtype)


def _moe_experts(xs, block_expert, n_used, w_gate_up, b_gate_up, w_down, b_down, tm, tn):
    r, d = xs.shape
    ne, _, two_de = w_gate_up.shape
    de = two_de // 2
    nb = r // tm
    nj = de // tn

    def blk(i, nu):
        return jnp.minimum(i, nu[0] - 1)

    act = pl.pallas_call(
        _gate_up_kernel,
        grid_spec=pltpu.PrefetchScalarGridSpec(
            num_scalar_prefetch=2, grid=(nj, nb),
            in_specs=[pl.BlockSpec((tm, d), lambda j, i, be, nu: (blk(i, nu), 0)),
                      pl.BlockSpec((None, d, tn), lambda j, i, be, nu: (be[blk(i, nu)], 0, j)),
                      pl.BlockSpec((None, d, tn), lambda j, i, be, nu: (be[blk(i, nu)], 0, nj + j)),
                      pl.BlockSpec((None, 1, tn), lambda j, i, be, nu: (be[blk(i, nu)], 0, j)),
                      pl.BlockSpec((None, 1, tn), lambda j, i, be, nu: (be[blk(i, nu)], 0, nj + j))],
            out_specs=pl.BlockSpec((tm, tn), lambda j, i, be, nu: (i, j)),
            scratch_shapes=[pltpu.VMEM((d, tn), BF16), pltpu.VMEM((d, tn), BF16)]),
        out_shape=jax.ShapeDtypeStruct((r, de), BF16),
        compiler_params=_cparams(("arbitrary", "arbitrary")), name="moe_gate_up")(
            block_expert, n_used, xs, w_gate_up, w_gate_up,
            b_gate_up.reshape(ne, 1, two_de), b_gate_up.reshape(ne, 1, two_de))

    dm = w_down.shape[2]
    njd = dm // tn
    return pl.pallas_call(
        _down_kernel,
        grid_spec=pltpu.PrefetchScalarGridSpec(
            num_scalar_prefetch=2, grid=(njd, nb),
            in_specs=[pl.BlockSpec((tm, de), lambda j, i, be, nu: (blk(i, nu), 0)),
                      pl.BlockSpec((None, de, tn), lambda j, i, be, nu: (be[blk(i, nu)], 0, j)),
                      pl.BlockSpec((None, 1, tn), lambda j, i, be, nu: (be[blk(i, nu)], 0, j))],
            out_specs=pl.BlockSpec((tm, tn), lambda j, i, be, nu: (i, j)),
            scratch_shapes=[pltpu.VMEM((de, tn), BF16)]),
        out_shape=jax.ShapeDtypeStruct((r, dm), F32),
        compiler_params=_cparams(("arbitrary", "arbitrary")), name="moe_down")(
            block_expert, n_used, act, w_down, b_down.reshape(ne, 1, dm))


def _combine_kernel(dest_ref, wgt_ref, x1_ref, mod_ref, g_ref, yb_hbm, op_ref, os_ref, buf, sem, *, nch):
    s = pl.program_id(0)

    def issue(n, c):
        t, k = n // TOP_K, n % TOP_K
        _row_copy(yb_hbm, buf.at[k], dest_ref[n], t, sem).start()
        return c

    def drain(n, c):
        t, k = n // TOP_K, n % TOP_K
        _row_copy(yb_hbm, buf.at[k], dest_ref[n], t, sem).wait()
        return c

    lax.fori_loop(0, CHUNK * TOP_K, issue, 0)
    lax.fori_loop(0, CHUNK * TOP_K, drain, 0)
    wgt = wgt_ref[...]
    y = wgt[:, 0:1] * buf[0]
    for k in range(1, TOP_K):
        y = y + wgt[:, k:k + 1] * buf[k]
    out = x1_ref[...] + mod_ref[5:6, :] * _rms(y, g_ref[...])

    @pl.when(s < nch)
    def _():
        op_ref[...] = out

    @pl.when(s == nch)
    def _():
        os_ref[...] = out


def _combine(dest, wgt, x1, mod, g, yb, nch, nb, seq):
    t, d = x1.shape
    return pl.pallas_call(
        functools.partial(_combine_kernel, nch=nch), grid=(nch + 1, nb),
        in_specs=[pl.BlockSpec((CHUNK * TOP_K,), lambda s, b: (s * nb + b,), memory_space=pltpu.SMEM),
                  _row_spec(LANES, nb), _row_spec(d, nb), _mod_spec(d, nch),
                  pl.BlockSpec((1, d), lambda s, b: (0, 0)),
                  pl.BlockSpec(memory_space=pl.ANY)],
        out_specs=[pl.BlockSpec((None, CHUNK, d),
                                lambda s, b: (jnp.where(s < nch, b, nb - 1), jnp.minimum(s, nch - 1), 0)),
                   pl.BlockSpec((None, CHUNK, d), lambda s, b: (jnp.where(s == nch, b, 0), 0, 0))],
        out_shape=[jax.ShapeDtypeStruct((nb, seq, d), F32), jax.ShapeDtypeStruct((nb, CHUNK, d), F32)],
        scratch_shapes=[pltpu.VMEM((TOP_K, CHUNK, d), F32), pltpu.SemaphoreType.DMA(())],
        compiler_params=_cparams(("arbitrary", "arbitrary")), name="moe_combine")(
            dest, wgt, x1, mod, g, yb)


def _layer(x_prompt, x_sample, c_prompt, c_sample, conv_s0, gdn_s0, gla_s0,
           w_ada, b_ada, n_mix_pre, n_mix_post, n_ffn_pre, n_ffn_post, w_in, conv_w, a_log, dt_bias,
           gdn_norm, gla_w2, gla_b, gla_norm, w_branch, w_out, w_router, b_router,
           w_gate_up, b_gate_up, w_down, b_down):
    nb, seq, d = x_prompt.shape
    assert x_sample.shape == (nb, CHUNK, d) and seq % CHUNK == 0
    nch = seq // CHUNK
    t = (nch + 1) * nb * CHUNK
    hv, dh = a_log.shape[0], gdn_norm.shape[0]
    cd = conv_w.shape[1]
    vd = hv * dh
    kd = (cd - vd) // 2
    hq = kd // dh
    nh, hk, hvd = gla_s0.shape[1:]
    kl, vl = nh * hk, nh * hvd
    rank = gla_w2.shape[0]
    ne = w_router.shape[1]
    assert 2 * hv + rank <= LANES and ne <= LANES

    c_all = jnp.concatenate([c_prompt, c_sample], axis=0)
    mod = _ada_proj(c_all, w_ada, b_ada).reshape(2, nb, N_MOD, d)

    sizes = (cd, hv, hv, vd, kl, kl, vl, rank, vl, 2 * d)
    offs = [0]
    for sz in sizes:
        offs.append(offs[-1] + sz)
    seg = lambda i: w_in[:, offs[i]:offs[i + 1]]
    main_ids = (0, 3, 4, 5, 6, 8, 9)
    w_main = jnp.concatenate([seg(i) for i in main_ids], axis=1).astype(BF16)
    n_small = 2 * hv + rank
    w_small = jnp.concatenate([seg(1), seg(2), seg(7), jnp.zeros((d, LANES - n_small), F32)], axis=1).astype(BF16)
    moff = [0]
    for i in main_ids:
        moff.append(moff[-1] + sizes[i])
    z_off, lq_off, lk_off, lv_off, lr_off, br_off = moff[1:7]
    assert z_off % vd == 0 and lq_off % kl == 0 and lk_off % kl == 0 and lv_off % vl == 0 and lr_off % vl == 0

    h1 = _prenorm(x_prompt, x_sample, mod, n_mix_pre.reshape(1, d), nch)
    tm = _tile(t, 512, CHUNK)
    p_main = _matmul(h1, w_main, tm, _tile(w_main.shape[1], 1024, LANES), F32, "in_proj")
    p_small = _matmul(h1, w_small, tm, LANES, F32, "in_proj_small")

    pad_lanes = lambda v: jnp.pad(v.reshape(1, -1), ((0, 0), (0, LANES - v.shape[0])))
    conv0 = jnp.pad(conv_s0, ((0, 0), (SUBLANES - (CONV_WIDTH - 1), 0), (0, 0)))
    oa, convp, convs, gdnp, gdns = _gdn(
        p_main, p_small, conv_w, conv0, pad_lanes(a_log), pad_lanes(dt_bias), gdn_norm.reshape(1, dh), gdn_s0,
        nch=nch, nb=nb, hq=hq, hv=hv, dh=dh, z_col=z_off // vd)
    w2_pad = jnp.zeros((LANES, kl), F32).at[2 * hv:2 * hv + rank].set(gla_w2)
    ob, glap, glas = _gla(
        p_main, p_small, w2_pad, gla_b.reshape(1, kl), gla_norm.reshape(1, hvd), gla_s0,
        nch=nch, nb=nb, nh=nh, hk=hk, hvd=hvd,
        cols=(lq_off // kl, lk_off // kl, lv_off // vl, lr_off // vl))

    tn = _tile(d, 1024, LANES)
    assert br_off % tn == 0
    ym = _merge(oa, ob, w_branch[:vd].astype(BF16), w_branch[vd:].astype(BF16), p_main, br_off, tm, tn)
    y2 = _matmul(ym, w_out.astype(BF16), tm, tn, F32, "out_proj")
    x1, h2 = _midnorm(x_prompt, x_sample, y2, mod, n_mix_post.reshape(1, d), n_ffn_pre.reshape(1, d), nch)

    w_r = jnp.pad(w_router, ((0, 0), (0, LANES - ne)))
    b_r = jnp.pad(b_router.reshape(1, ne), ((0, 0), (0, LANES - ne)), constant_values=NEG_BIG)
    idx, wgt, rnk, cnt = _router(h2, w_r, b_r, tm)
    tmm = _tile(t * TOP_K, 512, CHUNK)
    counts = cnt[0, :ne]
    padded = (counts + tmm - 1) // tmm * tmm
    pad_end = jnp.cumsum(padded)
    pad_start = pad_end - padded
    dest = (pad_start[idx[:, :TOP_K]] + rnk[:, :TOP_K]).astype(I32)
    n_blocks = t * TOP_K // tmm + ne
    n_rows = n_blocks * tmm
    token_of_pair = jnp.repeat(jnp.arange(t, dtype=I32), TOP_K)
    row_token = jnp.zeros((n_rows,), I32).at[dest.reshape(-1)].set(token_of_pair)
    block_start = jnp.arange(n_blocks, dtype=I32) * tmm
    block_expert = jnp.minimum(jnp.searchsorted(pad_end, block_start, side='right'), ne - 1).astype(I32)
    n_used = (pad_end[-1:] // tmm).astype(I32)

    xs = _dispatch(row_token, h2, tmm)
    yb = _moe_experts(xs, block_expert, n_used, w_gate_up, b_gate_up, w_down, b_down, tmm,
                      _tile(w_down.shape[2], 256, LANES))
    y_prompt, y_sample = _combine(dest.reshape(-1), wgt, x1, mod, n_ffn_post.reshape(1, d), yb, nch, nb, seq)
    tail = slice(SUBLANES - (CONV_WIDTH - 1), SUBLANES)
    return y_prompt, y_sample, convp[:, tail], gdnp, glap, convs[:, tail], gdns, glas


def kernel(x_prompt, x_sample, c_prompt, c_sample, state_conv, state_gdn, state_gla, w_ada, b_ada, norm_mix_pre, norm_mix_post, norm_ffn_pre, norm_ffn_post, w_in, conv_w, gdn_a_log, gdn_dt_bias, gdn_norm, gla_gate_w2, gla_gate_b, gla_norm, w_branch, w_out, w_router, b_router, w_gate_up, b_gate_up, w_down, b_down):
    assert w_ada.shape[0] == 1, "single-layer step"
    weights = (w_ada, b_ada, norm_mix_pre, norm_mix_post, norm_ffn_pre, norm_ffn_post, w_in, conv_w,
               gdn_a_log, gdn_dt_bias, gdn_norm, gla_gate_w2, gla_gate_b, gla_norm, w_branch, w_out,
               w_router, b_router, w_gate_up, b_gate_up, w_down, b_down)
    yp, ys, cp, gp, lp, cs, gs, ls = _layer(
        x_prompt, x_sample, c_prompt, c_sample, state_conv[0], state_gdn[0], state_gla[0],
        *(w[0] for w in weights))
    return yp, ys, cp[None], gp[None], lp[None], cs[None], gs[None], ls[None]
```

```python
import functools

import jax
import jax.numpy as jnp
from jax import lax
from jax.experimental import pallas as pl
from jax.experimental.pallas import tpu as pltpu

F32 = jnp.float32
BF16 = jnp.bfloat16
I32 = jnp.int32
HIGHEST = lax.Precision.HIGHEST

CHUNK = 64
EPS = 1e-6
CONV_WIDTH = 4
TOP_K = 4
N_MOD = 6
GLA_GATE_TEMP = 16.0
GLA_LOG_DECAY_FLOOR = -1.0
SWIGLU_LIMIT = 7.0
SWIGLU_ALPHA = 1.702
LANES = 128
SUBLANES = 8
NEG_BIG = -1e30
VMEM_LIMIT = 56 * 1024 * 1024


def _cparams(sem, row_gather=False):
    return pltpu.CompilerParams(dimension_semantics=sem, vmem_limit_bytes=VMEM_LIMIT,
                                disable_bounds_checks=row_gather)


def _tile(n, target, quantum):
    if n <= target:
        return n
    t = target - target % quantum
    while t > quantum and n % t:
        t -= quantum
    assert n % t == 0, (n, target, quantum)
    return t


def _dot(a, b, precision=None):
    return jnp.dot(a, b, preferred_element_type=F32, precision=precision)


def _dot_nt(a, b, precision=None):
    return lax.dot_general(a, b, (((1,), (1,)), ((), ())), preferred_element_type=F32,
                           precision=precision)


def _silu(x):
    return x * jax.nn.sigmoid(x)


def _softplus(x):
    return jnp.maximum(x, 0.0) + jnp.log1p(jnp.exp(-jnp.abs(x)))


def _rms(x, g):
    return x * lax.rsqrt(jnp.mean(x * x, axis=-1, keepdims=True) + EPS) * g


def _ada_kernel(c_ref, w_ref, b_ref, o_ref):
    a = _silu(c_ref[...]).astype(BF16)
    o_ref[...] = _dot(a, w_ref[...].astype(BF16)) + b_ref[...]


def _ada_proj(c, w, b):
    m, k = c.shape
    n = w.shape[1]
    tn = _tile(n, 512, LANES)
    return pl.pallas_call(
        _ada_kernel, grid=(n // tn,),
        in_specs=[pl.BlockSpec((m, k), lambda j: (0, 0)),
                  pl.BlockSpec((k, tn), lambda j: (0, j)),
                  pl.BlockSpec((1, tn), lambda j: (0, j))],
        out_specs=pl.BlockSpec((m, tn), lambda j: (0, j)),
        out_shape=jax.ShapeDtypeStruct((m, n), F32),
        compiler_params=_cparams(("arbitrary",)), name="ada_proj")(c, w, b.reshape(1, n))


def _mm_kernel(a_ref, w_ref, o_ref):
    o_ref[...] = _dot(a_ref[...], w_ref[...]).astype(o_ref.dtype)


def _matmul(a, w, tm, tn, out_dtype, name):
    m, k = a.shape
    n = w.shape[1]
    return pl.pallas_call(
        _mm_kernel, grid=(n // tn, m // tm),
        in_specs=[pl.BlockSpec((tm, k), lambda j, i: (i, 0)),
                  pl.BlockSpec((k, tn), lambda j, i: (0, j))],
        out_specs=pl.BlockSpec((tm, tn), lambda j, i: (i, j)),
        out_shape=jax.ShapeDtypeStruct((m, n), out_dtype),
        compiler_params=_cparams(("arbitrary", "arbitrary")), name=name)(a, w)


def _xp_spec(d, nch):
    return pl.BlockSpec((None, CHUNK, d), lambda s, b: (b, jnp.minimum(s, nch - 1), 0))


def _xs_spec(d, nch):
    return pl.BlockSpec((None, CHUNK, d), lambda s, b: (jnp.where(s < nch, 0, b), 0, 0))


def _mod_spec(d, nch):
    return pl.BlockSpec((None, None, N_MOD, d), lambda s, b: (jnp.where(s < nch, 0, 1), b, 0, 0))


def _row_spec(width, nb, col=0):
    return pl.BlockSpec((CHUNK, width), lambda s, b: (s * nb + b, col))


def _prenorm_kernel(xp_ref, xs_ref, mod_ref, g_ref, h_ref, *, nch):
    s = pl.program_id(0)
    x = jnp.where(s < nch, xp_ref[...], xs_ref[...])
    shift, scale = mod_ref[0:1, :], mod_ref[1:2, :]
    h_ref[...] = (_rms(x, g_ref[...]) * (1.0 + scale) + shift).astype(h_ref.dtype)


def _prenorm(xp, xs, mod, g, nch):
    nb, _, d = xp.shape
    t = (nch + 1) * nb * CHUNK
    return pl.pallas_call(
        functools.partial(_prenorm_kernel, nch=nch), grid=(nch + 1, nb),
        in_specs=[_xp_spec(d, nch), _xs_spec(d, nch), _mod_spec(d, nch),
                  pl.BlockSpec((1, d), lambda s, b: (0, 0))],
        out_specs=_row_spec(d, nb),
        out_shape=jax.ShapeDtypeStruct((t, d), BF16),
        compiler_params=_cparams(("arbitrary", "arbitrary")), name="prenorm")(xp, xs, mod, g)


def _midnorm_kernel(xp_ref, xs_ref, y_ref, mod_ref, g1_ref, g2_ref, x1_ref, h2_ref, *, nch):
    s = pl.program_id(0)
    x = jnp.where(s < nch, xp_ref[...], xs_ref[...])
    gate1, shift2, scale2 = mod_ref[2:3, :], mod_ref[3:4, :], mod_ref[4:5, :]
    x1 = x + gate1 * _rms(y_ref[...], g1_ref[...])
    x1_ref[...] = x1
    h2_ref[...] = _rms(x1, g2_ref[...]) * (1.0 + scale2) + shift2


def _midnorm(xp, xs, y, mod, g1, g2, nch):
    nb, _, d = xp.shape
    t = (nch + 1) * nb * CHUNK
    vec = pl.BlockSpec((1, d), lambda s, b: (0, 0))
    return pl.pallas_call(
        functools.partial(_midnorm_kernel, nch=nch), grid=(nch + 1, nb),
        in_specs=[_xp_spec(d, nch), _xs_spec(d, nch), _row_spec(d, nb), _mod_spec(d, nch), vec, vec],
        out_specs=[_row_spec(d, nb), _row_spec(d, nb)],
        out_shape=[jax.ShapeDtypeStruct((t, d), F32), jax.ShapeDtypeStruct((t, d), F32)],
        compiler_params=_cparams(("arbitrary", "arbitrary")), name="midnorm")(xp, xs, y, mod, g1, g2)


def _prompt_state_spec(block, nch, nb):
    tail = (0,) * (len(block) - 1)
    return pl.BlockSpec(block, lambda s, b: (jnp.where(s == nch - 1, b, jnp.where(s < nch - 1, 0, nb - 1)),) + tail)


def _sample_state_spec(block, nch):
    tail = (0,) * (len(block) - 1)
    return pl.BlockSpec(block, lambda s, b: (jnp.where(s == nch, b, 0),) + tail)


def _tri(n, strict=False, dtype=F32):
    i = lax.broadcasted_iota(I32, (n, n), 0)
    j = lax.broadcasted_iota(I32, (n, n), 1)
    return ((i > j) if strict else (i >= j)).astype(dtype)


GDN_HEAD_GROUP = 16
assert 2 * CHUNK == LANES


def _gdn_kernel(qkv_ref, z_ref, ab_ref, convw_ref, conv0_ref, alog_ref, dtb_ref, gn_ref, s0_ref,
                oa_ref, convp_ref, convs_ref, gdnp_ref, gdns_ref,
                tails, state, full, act, qn, kn, *, nch, hq, hv, dh, group):
    s = pl.program_id(0)
    b = pl.program_id(1)
    kd = hq * dh
    rep = hv // hq

    @pl.when(s == 0)
    def _():
        tails[b] = jnp.zeros(tails.shape[1:], F32)
        state[b] = jnp.zeros(state.shape[1:], F32)

    @pl.when(s == nch)
    def _():
        tails[b] = conv0_ref[...]
        state[b] = s0_ref[...]

    full[0:SUBLANES, :] = tails[b]
    full[SUBLANES:SUBLANES + CHUNK, :] = qkv_ref[...]
    base = SUBLANES - (CONV_WIDTH - 1)
    acc = full[base:base + CHUNK, :] * convw_ref[0:1, :]
    for i in range(1, CONV_WIDTH):
        acc = acc + full[base + i:base + i + CHUNK, :] * convw_ref[i:i + 1, :]
    act[...] = _silu(acc)
    new_tail = full[CHUNK:CHUNK + SUBLANES, :]
    tails[b] = new_tail

    for h in range(hq):
        q = act[:, h * dh:(h + 1) * dh]
        k = act[:, kd + h * dh:kd + (h + 1) * dh]
        qn[:, h * dh:(h + 1) * dh] = q * lax.rsqrt(jnp.sum(q * q, axis=-1, keepdims=True) + EPS) * dh ** -0.5
        kn[:, h * dh:(h + 1) * dh] = k * lax.rsqrt(jnp.sum(k * k, axis=-1, keepdims=True) + EPS)

    ab = ab_ref[...]
    g_all = -jnp.exp(alog_ref[...]) * _softplus(ab + dtb_ref[...])
    beta_all = jax.nn.sigmoid(ab)
    gc_all = _dot(_tri(CHUNK), g_all, HIGHEST)
    gc_t = jnp.concatenate([gc_all, gc_all], axis=0).T
    eg_all = jnp.exp(gc_all)
    g_last = gc_all[CHUNK - 1:CHUNK, :]
    ekd_all = jnp.exp(g_last - gc_all)
    edec_all = jnp.exp(g_last)

    ii = lax.broadcasted_iota(I32, (CHUNK, LANES), 0)
    lane = lax.broadcasted_iota(I32, (CHUNK, LANES), 1)
    first = lane < CHUNK
    jj = jnp.where(first, lane, lane - CHUNK)
    incl = ii >= jj
    strict = ii > jj
    eye = (ii == jj).astype(F32)
    zeros_bf = jnp.zeros((CHUNK, LANES), BF16)

    def halves(x, with_rhs=True):
        hi = x.astype(BF16)
        rest = x - hi.astype(F32)
        rhs = None
        if with_rhs:
            rhs = jnp.concatenate([jnp.where(first, x, rest).astype(BF16), zeros_bf], axis=0)
        return hi, rest.astype(BF16), rhs

    def fold(r):
        sm = r[:CHUNK] + r[CHUNK:]
        return sm + pltpu.roll(sm, CHUNK, axis=1)

    for h0 in range(0, hv, group):
        heads = range(h0, min(h0 + group, hv))
        kh, qh, kb, kcat, dec, nn, tm = {}, {}, {}, {}, {}, {}, {}
        for h in heads:
            kh[h] = kn[:, (h // rep) * dh:(h // rep + 1) * dh]
            qh[h] = qn[:, (h // rep) * dh:(h // rep + 1) * dh]
            kcat[h] = jnp.concatenate([kh[h], kh[h]], axis=0).astype(BF16)
            beta = beta_all[:, hv + h:hv + h + 1]
            kb[h] = kh[h] * beta
            diff = gc_all[:, h:h + 1] - gc_t[h:h + 1, :]
            dec[h] = jnp.exp(jnp.where(incl, diff, NEG_BIG))
        for h in heads:
            a = _dot_nt(kb[h].astype(BF16), kcat[h])
            nn[h] = jnp.where(strict, -(a * dec[h]), 0.0)
            tm[h] = eye + nn[h]
        for h in heads:
            p_hi, p_lo, rhs = halves(nn[h])
            nn[h] = fold(_dot(jnp.concatenate([p_hi, p_lo], axis=0), rhs))
        p = 4
        while p < CHUNK:
            for h in heads:
                p_hi, p_lo, rhs = halves(nn[h])
                t_hi, t_lo, _ = halves(tm[h], with_rhs=False)
                r = _dot(jnp.concatenate([p_hi, p_lo, t_hi, t_lo], axis=0), rhs)
                nn[h] = fold(r[:2 * CHUNK])
                tm[h] = tm[h] + fold(r[2 * CHUNK:])
            p *= 2
        for h in heads:
            _, _, rhs = halves(nn[h])
            t_hi, t_lo, _ = halves(tm[h], with_rhs=False)
            tm[h] = tm[h] + fold(_dot(jnp.concatenate([t_hi, t_lo], axis=0), rhs))
        u, w, attn = {}, {}, {}
        for h in heads:
            beta = beta_all[:, hv + h:hv + h + 1]
            vh = act[:, 2 * kd + h * dh:2 * kd + (h + 1) * dh]
            rhs = jnp.concatenate([vh * beta, kb[h] * eg_all[:, h:h + 1]], axis=-1).astype(BF16)
            sol = _dot(tm[h][:, :CHUNK].astype(BF16), rhs)
            u[h], w[h] = sol[:, :dh], sol[:, dh:]
            attn[h] = (_dot_nt(qh[h].astype(BF16), kcat[h]) * dec[h])[:, :CHUNK].astype(BF16)
        vnew, o = {}, {}
        for h in heads:
            lhs = jnp.concatenate([w[h], qh[h] * eg_all[:, h:h + 1]], axis=0).astype(BF16)
            ws_qs = _dot(lhs, state[b, h].astype(BF16))
            vnew[h] = u[h] - ws_qs[:CHUNK]
            o[h] = ws_qs[CHUNK:] + _dot(attn[h], vnew[h].astype(BF16))
        for h in heads:
            kdec_t = (kh[h] * ekd_all[:, h:h + 1]).T.astype(BF16)
            state[b, h] = state[b, h] * edec_all[0:1, h:h + 1] + _dot(kdec_t, vnew[h].astype(BF16))
        for h in heads:
            zh = z_ref[:, h * dh:(h + 1) * dh]
            oa_ref[:, h * dh:(h + 1) * dh] = (_rms(o[h], gn_ref[...]) * _silu(zh)).astype(oa_ref.dtype)

    @pl.when(s == nch - 1)
    def _():
        convp_ref[...] = new_tail
        gdnp_ref[...] = state[b]

    @pl.when(s == nch)
    def _():
        convs_ref[...] = new_tail
        gdns_ref[...] = state[b]


def _gdn(p_main, p_small, conv_w, conv0, a_log, dt_bias, gdn_norm, s0, *, nch, nb, hq, hv, dh, z_col):
    t = p_main.shape[0]
    cd = conv_w.shape[1]
    vd = hv * dh
    kd = hq * dh
    vec = lambda n: pl.BlockSpec((1, n), lambda s, b: (0, 0))
    st_block = (None, hv, dh, dh)
    cv_block = (None, SUBLANES, cd)
    return pl.pallas_call(
        functools.partial(_gdn_kernel, nch=nch, hq=hq, hv=hv, dh=dh, group=GDN_HEAD_GROUP), grid=(nch + 1, nb),
        in_specs=[_row_spec(cd, nb), _row_spec(vd, nb, z_col), _row_spec(LANES, nb),
                  pl.BlockSpec((CONV_WIDTH, cd), lambda s, b: (0, 0)),
                  _sample_state_spec(cv_block, nch), vec(LANES), vec(LANES), vec(dh),
                  _sample_state_spec(st_block, nch)],
        out_specs=[_row_spec(vd, nb),
                   _prompt_state_spec(cv_block, nch, nb), _sample_state_spec(cv_block, nch),
                   _prompt_state_spec(st_block, nch, nb), _sample_state_spec(st_block, nch)],
        out_shape=[jax.ShapeDtypeStruct((t, vd), BF16),
                   jax.ShapeDtypeStruct((nb, SUBLANES, cd), F32), jax.ShapeDtypeStruct((nb, SUBLANES, cd), F32),
                   jax.ShapeDtypeStruct((nb, hv, dh, dh), F32), jax.ShapeDtypeStruct((nb, hv, dh, dh), F32)],
        scratch_shapes=[pltpu.VMEM((nb, SUBLANES, cd), F32), pltpu.VMEM((nb, hv, dh, dh), F32),
                        pltpu.VMEM((SUBLANES + CHUNK, cd), F32), pltpu.VMEM((CHUNK, cd), F32),
                        pltpu.VMEM((CHUNK, kd), F32), pltpu.VMEM((CHUNK, kd), F32)],
        compiler_params=_cparams(("arbitrary", "arbitrary")), name="gdn")(
            p_main, p_main, p_small, conv_w, conv0, a_log, dt_bias, gdn_norm, s0)


def _gla_kernel(q_ref, k_ref, v_ref, r_ref, ab_ref, w2_ref, gb_ref, gn_ref, s0_ref,
                ob_ref, glap_ref, glas_ref, state, *, nch, nh, hk, hvd):
    s = pl.program_id(0)
    b = pl.program_id(1)

    @pl.when(s == 0)
    def _():
        state[b] = jnp.zeros(state.shape[1:], F32)

    @pl.when(s == nch)
    def _():
        state[b] = s0_ref[...]

    pre = _dot(ab_ref[...], w2_ref[...], HIGHEST) + gb_ref[...]
    log_sig = jnp.minimum(pre, 0.0) - jnp.log1p(jnp.exp(-jnp.abs(pre)))
    gk = jnp.maximum(log_sig / GLA_GATE_TEMP, GLA_LOG_DECAY_FLOOR)
    bc = _dot(_tri(CHUNK), gk, HIGHEST)
    b_last = bc[CHUNK - 1:CHUNK, :]
    qd = (q_ref[...] * hk ** -0.5 * jnp.exp(bc)).astype(BF16)
    ki = (k_ref[...] * jnp.exp(-bc)).astype(BF16)
    kdec = k_ref[...] * jnp.exp(b_last - bc)
    edec = jnp.exp(b_last)
    incl = _tri(CHUNK) > 0.0

    for h in range(nh):
        ks = slice(h * hk, (h + 1) * hk)
        vs = slice(h * hvd, (h + 1) * hvd)
        vh = v_ref[:, vs].astype(BF16)
        attn = jnp.where(incl, _dot_nt(qd[:, ks], ki[:, ks]), 0.0).astype(BF16)
        o = _dot(attn, vh) + _dot(qd[:, ks], state[b, h].astype(BF16))
        edec_col = jnp.broadcast_to(edec[:, ks], (SUBLANES, hk)).T[:, 0:1]
        state[b, h] = state[b, h] * edec_col + _dot(kdec[:, ks].T.astype(BF16), vh)
        ob_ref[:, vs] = (_rms(o, gn_ref[...]) * _silu(r_ref[:, vs])).astype(ob_ref.dtype)

    @pl.when(s == nch - 1)
    def _():
        glap_ref[...] = state[b]

    @pl.when(s == nch)
    def _():
        glas_ref[...] = state[b]


def _gla(p_main, p_small, w2_pad, gate_b, gla_norm, s0, *, nch, nb, nh, hk, hvd, cols):
    t = p_main.shape[0]
    kl, vl = nh * hk, nh * hvd
    st_block = (None, nh, hk, hvd)
    q_col, k_col, v_col, r_col = cols
    return pl.pallas_call(
        functools.partial(_gla_kernel, nch=nch, nh=nh, hk=hk, hvd=hvd), grid=(nch + 1, nb),
        in_specs=[_row_spec(kl, nb, q_col), _row_spec(kl, nb, k_col), _row_spec(vl, nb, v_col),
                  _row_spec(vl, nb, r_col), _row_spec(LANES, nb),
                  pl.BlockSpec((LANES, kl), lambda s, b: (0, 0)),
                  pl.BlockSpec((1, kl), lambda s, b: (0, 0)),
                  pl.BlockSpec((1, hvd), lambda s, b: (0, 0)),
                  _sample_state_spec(st_block, nch)],
        out_specs=[_row_spec(vl, nb), _prompt_state_spec(st_block, nch, nb), _sample_state_spec(st_block, nch)],
        out_shape=[jax.ShapeDtypeStruct((t, vl), BF16),
                   jax.ShapeDtypeStruct((nb, nh, hk, hvd), F32), jax.ShapeDtypeStruct((nb, nh, hk, hvd), F32)],
        scratch_shapes=[pltpu.VMEM((nb, nh, hk, hvd), F32)],
        compiler_params=_cparams(("arbitrary", "arbitrary")), name="gla")(
            p_main, p_main, p_main, p_main, p_small, w2_pad, gate_b, gla_norm, s0)


def _merge_kernel(oa_ref, ob_ref, wa_ref, wb_ref, g0_ref, g1_ref, o_ref):
    ya = _dot(oa_ref[...], wa_ref[...])
    yb = _dot(ob_ref[...], wb_ref[...])
    o_ref[...] = (jax.nn.sigmoid(g0_ref[...]) * ya + jax.nn.sigmoid(g1_ref[...]) * yb).astype(o_ref.dtype)


def _merge(oa, ob, wa, wb, p_main, br_off, tm, tn):
    t, ka = oa.shape
    kb = ob.shape[1]
    d = wa.shape[1]
    c0, c1 = br_off // tn, (br_off + d) // tn
    return pl.pallas_call(
        _merge_kernel, grid=(d // tn, t // tm),
        in_specs=[pl.BlockSpec((tm, ka), lambda j, i: (i, 0)), pl.BlockSpec((tm, kb), lambda j, i: (i, 0)),
                  pl.BlockSpec((ka, tn), lambda j, i: (0, j)), pl.BlockSpec((kb, tn), lambda j, i: (0, j)),
                  pl.BlockSpec((tm, tn), lambda j, i: (i, c0 + j)), pl.BlockSpec((tm, tn), lambda j, i: (i, c1 + j))],
        out_specs=pl.BlockSpec((tm, tn), lambda j, i: (i, j)),
        out_shape=jax.ShapeDtypeStruct((t, d), BF16),
        compiler_params=_cparams(("arbitrary", "arbitrary")), name="merge")(oa, ob, wa, wb, p_main, p_main)


def _router_kernel(h_ref, w_ref, b_ref, idx_ref, wgt_ref, rank_ref, cnt_ref, run, *, tm):
    @pl.when(pl.program_id(0) == 0)
    def _():
        run[...] = jnp.zeros(run.shape, F32)

    logits = _dot(h_ref[...], w_ref[...], HIGHEST) + b_ref[...]
    lane = lax.broadcasted_iota(I32, logits.shape, 1).astype(F32)
    cur = logits
    tops, sels, idxs = [], [], []
    for _ in range(TOP_K):
        m = jnp.max(cur, axis=-1, keepdims=True)
        idx = jnp.min(jnp.where(cur == m, lane, float(LANES)), axis=-1, keepdims=True)
        sel = lane == idx
        tops.append(m)
        sels.append(sel)
        idxs.append(idx)
        cur = jnp.where(sel, 2.0 * NEG_BIG, cur)
    exps = [jnp.exp(m - tops[0]) for m in tops]
    denom = exps[0]
    for e in exps[1:]:
        denom = denom + e
    onehot = sels[0].astype(F32)
    for sel in sels[1:]:
        onehot = onehot + sel.astype(F32)
    prefix = _dot(_tri(tm, strict=True, dtype=BF16), onehot.astype(BF16)) + run[0:1, :]
    idx_out = jnp.zeros(logits.shape, F32)
    wgt_out = jnp.zeros(logits.shape, F32)
    rank_out = jnp.zeros(logits.shape, F32)
    for k in range(TOP_K):
        rank = jnp.sum(jnp.where(sels[k], prefix, 0.0), axis=-1, keepdims=True)
        idx_out = jnp.where(lane == float(k), idxs[k], idx_out)
        wgt_out = jnp.where(lane == float(k), exps[k] / denom, wgt_out)
        rank_out = jnp.where(lane == float(k), rank, rank_out)
    idx_ref[...] = idx_out.astype(I32)
    wgt_ref[...] = wgt_out
    rank_ref[...] = rank_out.astype(I32)
    total = run[...] + jnp.sum(onehot, axis=0, keepdims=True)
    run[...] = total
    cnt_ref[...] = total.astype(I32)


def _router(h2, w_pad, b_pad, tm):
    t, d = h2.shape
    row = pl.BlockSpec((tm, LANES), lambda i: (i, 0))
    return pl.pallas_call(
        functools.partial(_router_kernel, tm=tm), grid=(t // tm,),
        in_specs=[pl.BlockSpec((tm, d), lambda i: (i, 0)), pl.BlockSpec((d, LANES), lambda i: (0, 0)),
                  pl.BlockSpec((1, LANES), lambda i: (0, 0))],
        out_specs=[row, row, row, pl.BlockSpec((SUBLANES, LANES), lambda i: (0, 0))],
        out_shape=[jax.ShapeDtypeStruct((t, LANES), I32), jax.ShapeDtypeStruct((t, LANES), F32),
                   jax.ShapeDtypeStruct((t, LANES), I32), jax.ShapeDtypeStruct((SUBLANES, LANES), I32)],
        scratch_shapes=[pltpu.VMEM((SUBLANES, LANES), F32)],
        compiler_params=_cparams(("arbitrary",)), name="router")(h2, w_pad, b_pad)


def _row_copy(src_hbm, dst, row, slot, sem):
    return pltpu.make_async_copy(src_hbm.at[pl.ds(row, 1)], dst.at[pl.ds(slot, 1)], sem)


GATHER_UNROLL = 8


def _dispatch_kernel(nu_ref, tok_ref, src_hbm, o_ref, buf, sem, *, tg):
    i = pl.program_id(0)

    @pl.when(i < nu_ref[0])
    def _():
        def issue(r, c):
            _row_copy(src_hbm, buf, tok_ref[r], r, sem).start()
            return c

        def drain(r, c):
            _row_copy(src_hbm, buf, tok_ref[r], r, sem).wait()
            return c

        lax.fori_loop(0, tg, issue, 0, unroll=GATHER_UNROLL)
        lax.fori_loop(0, tg, drain, 0, unroll=GATHER_UNROLL)
        o_ref[...] = buf[...].astype(o_ref.dtype)

    @pl.when(i >= nu_ref[0])
    def _():
        o_ref[...] = jnp.zeros(o_ref.shape, o_ref.dtype)


def _dispatch(row_token, n_used, h2, tg):
    r = row_token.shape[0]
    d = h2.shape[1]
    return pl.pallas_call(
        functools.partial(_dispatch_kernel, tg=tg),
        grid_spec=pltpu.PrefetchScalarGridSpec(
            num_scalar_prefetch=1, grid=(r // tg,),
            in_specs=[pl.BlockSpec((tg,), lambda i, nu: (i,), memory_space=pltpu.SMEM),
                      pl.BlockSpec(memory_space=pl.ANY)],
            out_specs=pl.BlockSpec((tg, d), lambda i, nu: (i, 0)),
            scratch_shapes=[pltpu.VMEM((tg, d), F32), pltpu.SemaphoreType.DMA(())]),
        out_shape=jax.ShapeDtypeStruct((r, d), BF16),
        compiler_params=_cparams(("arbitrary",), row_gather=True), name="moe_dispatch")(
            n_used, row_token, h2)


def _expert_changed(be_ref, i):
    return jnp.logical_or(i == 0, be_ref[i] != be_ref[jnp.maximum(i - 1, 0)])


def _gate_up_kernel(be_ref, nu_ref, x_ref, wg_ref, wl_ref, bg_ref, bl_ref, o_ref, wg_bf, wl_bf):
    i = pl.program_id(1)

    @pl.when(i < nu_ref[0])
    def _():
        @pl.when(_expert_changed(be_ref, i))
        def _():
            wg_bf[...] = wg_ref[...].astype(BF16)
            wl_bf[...] = wl_ref[...].astype(BF16)

        x = x_ref[...]
        glu = jnp.minimum(_dot(x, wg_bf[...]) + bg_ref[...], SWIGLU_LIMIT)
        lin = jnp.clip(_dot(x, wl_bf[...]) + bl_ref[...], -SWIGLU_LIMIT, SWIGLU_LIMIT)
        o_ref[...] = (glu * jax.nn.sigmoid(SWIGLU_ALPHA * glu) * (lin + 1.0)).astype(o_ref.dtype)

    @pl.when(i >= nu_ref[0])
    def _():
        o_ref[...] = jnp.zeros(o_ref.shape, o_ref.dtype)


def _down_kernel(be_ref, nu_ref, a_ref, w_ref, b_ref, o_ref, w_bf):
    i = pl.program_id(1)

    @pl.when(i < nu_ref[0])
    def _():
        @pl.when(_expert_changed(be_ref, i))
        def _():
            w_bf[...] = w_ref[...].astype(BF16)

        o_ref[...] = _dot(a_ref[...], w_bf[...]) + b_ref[...]

    @pl.when(i >= nu_ref[0])
    def _():
        o_ref[...] = jnp.zeros(o_ref.shape, o_ref.dtype)


def _moe_experts(xs, block_expert, n_used, w_gate_up, b_gate_up, w_down, b_down, tm):
    r, d = xs.shape
    ne, _, two_de = w_gate_up.shape
    de = two_de // 2
    nb = r // tm
    tn = _tile(de, 512, LANES)
    nj = de // tn

    def blk(i, nu):
        return jnp.minimum(i, nu[0] - 1)

    act = pl.pallas_call(
        _gate_up_kernel,
        grid_spec=pltpu.PrefetchScalarGridSpec(
            num_scalar_prefetch=2, grid=(nj, nb),
            in_specs=[pl.BlockSpec((tm, d), lambda j, i, be, nu: (blk(i, nu), 0)),
                      pl.BlockSpec((None, d, tn), lambda j, i, be, nu: (be[blk(i, nu)], 0, j)),
                      pl.BlockSpec((None, d, tn), lambda j, i, be, nu: (be[blk(i, nu)], 0, nj + j)),
                      pl.BlockSpec((None, 1, tn), lambda j, i, be, nu: (be[blk(i, nu)], 0, j)),
                      pl.BlockSpec((None, 1, tn), lambda j, i, be, nu: (be[blk(i, nu)], 0, nj + j))],
            out_specs=pl.BlockSpec((tm, tn), lambda j, i, be, nu: (i, j)),
            scratch_shapes=[pltpu.VMEM((d, tn), BF16), pltpu.VMEM((d, tn), BF16)]),
        out_shape=jax.ShapeDtypeStruct((r, de), BF16),
        compiler_params=_cparams(("arbitrary", "arbitrary")), name="moe_gate_up")(
            block_expert, n_used, xs, w_gate_up, w_gate_up,
            b_gate_up.reshape(ne, 1, two_de), b_gate_up.reshape(ne, 1, two_de))

    dm = w_down.shape[2]
    tn = _tile(dm, 1024, LANES)
    njd = dm // tn
    return pl.pallas_call(
        _down_kernel,
        grid_spec=pltpu.PrefetchScalarGridSpec(
            num_scalar_prefetch=2, grid=(njd, nb),
            in_specs=[pl.BlockSpec((tm, de), lambda j, i, be, nu: (blk(i, nu), 0)),
                      pl.BlockSpec((None, de, tn), lambda j, i, be, nu: (be[blk(i, nu)], 0, j)),
                      pl.BlockSpec((None, 1, tn), lambda j, i, be, nu: (be[blk(i, nu)], 0, j))],
            out_specs=pl.BlockSpec((tm, tn), lambda j, i, be, nu: (i, j)),
            scratch_shapes=[pltpu.VMEM((de, tn), BF16)]),
        out_shape=jax.ShapeDtypeStruct((r, dm), F32),
        compiler_params=_cparams(("arbitrary", "arbitrary")), name="moe_down")(
            block_expert, n_used, act, w_down, b_down.reshape(ne, 1, dm))


def _combine_kernel(dest_ref, wgt_ref, x1_ref, mod_ref, g_ref, yb_hbm, op_ref, os_ref, buf, sem, *, nch):
    s = pl.program_id(0)

    def issue(n, c):
        t, k = n // TOP_K, n % TOP_K
        _row_copy(yb_hbm, buf.at[k], dest_ref[n], t, sem).start()
        return c

    def drain(n, c):
        t, k = n // TOP_K, n % TOP_K
        _row_copy(yb_hbm, buf.at[k], dest_ref[n], t, sem).wait()
        return c

    lax.fori_loop(0, CHUNK * TOP_K, issue, 0, unroll=GATHER_UNROLL)
    lax.fori_loop(0, CHUNK * TOP_K, drain, 0, unroll=GATHER_UNROLL)
    wgt = wgt_ref[...]
    y = wgt[:, 0:1] * buf[0]
    for k in range(1, TOP_K):
        y = y + wgt[:, k:k + 1] * buf[k]
    out = x1_ref[...] + mod_ref[5:6, :] * _rms(y, g_ref[...])

    @pl.when(s < nch)
    def _():
        op_ref[...] = out

    @pl.when(s == nch)
    def _():
        os_ref[...] = out


def _combine(dest, wgt, x1, mod, g, yb, nch, nb, seq):
    t, d = x1.shape
    return pl.pallas_call(
        functools.partial(_combine_kernel, nch=nch), grid=(nch + 1, nb),
        in_specs=[pl.BlockSpec((CHUNK * TOP_K,), lambda s, b: (s * nb + b,), memory_space=pltpu.SMEM),
                  _row_spec(LANES, nb), _row_spec(d, nb), _mod_spec(d, nch),
                  pl.BlockSpec((1, d), lambda s, b: (0, 0)),
                  pl.BlockSpec(memory_space=pl.ANY)],
        out_specs=[pl.BlockSpec((None, CHUNK, d),
                                lambda s, b: (jnp.where(s < nch, b, nb - 1), jnp.minimum(s, nch - 1), 0)),
                   pl.BlockSpec((None, CHUNK, d), lambda s, b: (jnp.where(s == nch, b, 0), 0, 0))],
        out_shape=[jax.ShapeDtypeStruct((nb, seq, d), F32), jax.ShapeDtypeStruct((nb, CHUNK, d), F32)],
        scratch_shapes=[pltpu.VMEM((TOP_K, CHUNK, d), F32), pltpu.SemaphoreType.DMA(())],
        compiler_params=_cparams(("arbitrary", "arbitrary"), row_gather=True), name="moe_combine")(
            dest, wgt, x1, mod, g, yb)


def _layer(x_prompt, x_sample, c_prompt, c_sample, conv_s0, gdn_s0, gla_s0,
           w_ada, b_ada, n_mix_pre, n_mix_post, n_ffn_pre, n_ffn_post, w_in, conv_w, a_log, dt_bias,
           gdn_norm, gla_w2, gla_b, gla_norm, w_branch, w_out, w_router, b_router,
           w_gate_up, b_gate_up, w_down, b_down):
    nb, seq, d = x_prompt.shape
    assert x_sample.shape == (nb, CHUNK, d) and seq % CHUNK == 0
    nch = seq // CHUNK
    t = (nch + 1) * nb * CHUNK
    hv, dh = a_log.shape[0], gdn_norm.shape[0]
    cd = conv_w.shape[1]
    vd = hv * dh
    kd = (cd - vd) // 2
    hq = kd // dh
    nh, hk, hvd = gla_s0.shape[1:]
    kl, vl = nh * hk, nh * hvd
    rank = gla_w2.shape[0]
    ne = w_router.shape[1]
    assert 2 * hv + rank <= LANES and ne <= LANES

    c_all = jnp.concatenate([c_prompt, c_sample], axis=0)
    mod = _ada_proj(c_all, w_ada, b_ada).reshape(2, nb, N_MOD, d)

    sizes = (cd, hv, hv, vd, kl, kl, vl, rank, vl, 2 * d)
    offs = [0]
    for sz in sizes:
        offs.append(offs[-1] + sz)
    seg = lambda i: w_in[:, offs[i]:offs[i + 1]]
    main_ids = (0, 3, 4, 5, 6, 8, 9)
    w_main = jnp.concatenate([seg(i) for i in main_ids], axis=1).astype(BF16)
    n_small = 2 * hv + rank
    w_small = jnp.concatenate([seg(1), seg(2), seg(7), jnp.zeros((d, LANES - n_small), F32)], axis=1).astype(BF16)
    moff = [0]
    for i in main_ids:
        moff.append(moff[-1] + sizes[i])
    z_off, lq_off, lk_off, lv_off, lr_off, br_off = moff[1:7]
    assert z_off % vd == 0 and lq_off % kl == 0 and lk_off % kl == 0 and lv_off % vl == 0 and lr_off % vl == 0

    h1 = _prenorm(x_prompt, x_sample, mod, n_mix_pre.reshape(1, d), nch)
    tm = _tile(t, 512, CHUNK)
    p_main = _matmul(h1, w_main, tm, _tile(w_main.shape[1], 1024, LANES), F32, "in_proj")
    p_small = _matmul(h1, w_small, tm, LANES, F32, "in_proj_small")

    pad_lanes = lambda v: jnp.pad(v.reshape(1, -1), ((0, 0), (0, LANES - v.shape[0])))
    conv0 = jnp.pad(conv_s0, ((0, 0), (SUBLANES - (CONV_WIDTH - 1), 0), (0, 0)))
    oa, convp, convs, gdnp, gdns = _gdn(
        p_main, p_small, conv_w, conv0, pad_lanes(a_log), pad_lanes(dt_bias), gdn_norm.reshape(1, dh), gdn_s0,
        nch=nch, nb=nb, hq=hq, hv=hv, dh=dh, z_col=z_off // vd)
    w2_pad = jnp.zeros((LANES, kl), F32).at[2 * hv:2 * hv + rank].set(gla_w2)
    ob, glap, glas = _gla(
        p_main, p_small, w2_pad, gla_b.reshape(1, kl), gla_norm.reshape(1, hvd), gla_s0,
        nch=nch, nb=nb, nh=nh, hk=hk, hvd=hvd,
        cols=(lq_off // kl, lk_off // kl, lv_off // vl, lr_off // vl))

    tn = _tile(d, 1024, LANES)
    assert br_off % tn == 0
    ym = _merge(oa, ob, w_branch[:vd].astype(BF16), w_branch[vd:].astype(BF16), p_main, br_off, tm, tn)
    y2 = _matmul(ym, w_out.astype(BF16), tm, tn, F32, "out_proj")
    x1, h2 = _midnorm(x_prompt, x_sample, y2, mod, n_mix_post.reshape(1, d), n_ffn_pre.reshape(1, d), nch)

    w_r = jnp.pad(w_router, ((0, 0), (0, LANES - ne)))
    b_r = jnp.pad(b_router.reshape(1, ne), ((0, 0), (0, LANES - ne)), constant_values=NEG_BIG)
    idx, wgt, rnk, cnt = _router(h2, w_r, b_r, tm)
    tmm = _tile(t * TOP_K, 512, CHUNK)
    counts = cnt[0, :ne]
    padded = (counts + tmm - 1) // tmm * tmm
    pad_end = jnp.cumsum(padded)
    pad_start = pad_end - padded
    dest = (pad_start[idx[:, :TOP_K]] + rnk[:, :TOP_K]).astype(I32)
    n_blocks = t * TOP_K // tmm + ne
    n_rows = n_blocks * tmm
    token_of_pair = jnp.repeat(jnp.arange(t, dtype=I32), TOP_K)
    row_token = jnp.zeros((n_rows,), I32).at[dest.reshape(-1)].set(token_of_pair)
    block_start = jnp.arange(n_blocks, dtype=I32) * tmm
    block_expert = jnp.minimum(jnp.sum(block_start[:, None] >= pad_end[None, :], axis=1), ne - 1).astype(I32)
    n_used = (pad_end[-1:] // tmm).astype(I32)

    xs = _dispatch(row_token, n_used, h2, tmm)
    yb = _moe_experts(xs, block_expert, n_used, w_gate_up, b_gate_up, w_down, b_down, tmm)
    y_prompt, y_sample = _combine(dest.reshape(-1), wgt, x1, mod, n_ffn_post.reshape(1, d), yb, nch, nb, seq)
    tail = slice(SUBLANES - (CONV_WIDTH - 1), SUBLANES)
    return y_prompt, y_sample, convp[:, tail], gdnp, glap, convs[:, tail], gdns, glas


def kernel(x_prompt, x_sample, c_prompt, c_sample, state_conv, state_gdn, state_gla, w_ada, b_ada, norm_mix_pre, norm_mix_post, norm_ffn_pre, norm_ffn_post, w_in, conv_w, gdn_a_log, gdn_dt_bias, gdn_norm, gla_gate_w2, gla_gate_b, gla_norm, w_branch, w_out, w_router, b_router, w_gate_up, b_gate_up, w_down, b_down):
    assert w_ada.shape[0] == 1, "single-layer step"
    weights = (w_ada, b_ada, norm_mix_pre, norm_mix_post, norm_ffn_pre, norm_ffn_post, w_in, conv_w,
               gdn_a_log, gdn_dt_bias, gdn_norm, gla_gate_w2, gla_gate_b, gla_norm, w_branch, w_out,
               w_router, b_router, w_gate_up, b_gate_up, w_down, b_down)
    yp, ys, cp, gp, lp, cs, gs, ls = _layer(
        x_prompt, x_sample, c_prompt, c_sample, state_conv[0], state_gdn[0], state_gla[0],
        *(w[0] for w in weights))
    return yp, ys, cp[None], gp[None], lp[None], cs[None], gs[None], ls[None]
```

```python
import functools

import jax
import jax.numpy as jnp
from jax import lax
from jax.experimental import pallas as pl
from jax.experimental.pallas import tpu as pltpu

F32 = jnp.float32
BF16 = jnp.bfloat16
I32 = jnp.int32
HIGHEST = lax.Precision.HIGHEST

CHUNK = 64
EPS = 1e-6
CONV_WIDTH = 4
TOP_K = 4
N_MOD = 6
GLA_GATE_TEMP = 16.0
GLA_LOG_DECAY_FLOOR = -1.0
SWIGLU_LIMIT = 7.0
SWIGLU_ALPHA = 1.702
LANES = 128
SUBLANES = 8
NEG_BIG = -1e30
VMEM_LIMIT = 60 * 1024 * 1024


def _cparams(sem, row_gather=False):
    return pltpu.CompilerParams(dimension_semantics=sem, vmem_limit_bytes=VMEM_LIMIT,
                                disable_bounds_checks=row_gather)


def _tile(n, target, quantum):
    if n <= target:
        return n
    t = target - target % quantum
    while t > quantum and n % t:
        t -= quantum
    assert n % t == 0, (n, target, quantum)
    return t


def _dot(a, b, precision=None):
    return jnp.dot(a, b, preferred_element_type=F32, precision=precision)


def _dot_nt(a, b, precision=None):
    return lax.dot_general(a, b, (((1,), (1,)), ((), ())), preferred_element_type=F32,
                           precision=precision)


def _silu(x):
    return x * jax.nn.sigmoid(x)


def _softplus(x):
    return jnp.maximum(x, 0.0) + jnp.log1p(jnp.exp(-jnp.abs(x)))


def _rms(x, g):
    return x * lax.rsqrt(jnp.mean(x * x, axis=-1, keepdims=True) + EPS) * g


def _ada_kernel(c_ref, w_ref, b_ref, o_ref):
    a = _silu(c_ref[...]).astype(BF16)
    o_ref[...] = _dot(a, w_ref[...].astype(BF16)) + b_ref[...]


def _ada_proj(c, w, b):
    m, k = c.shape
    n = w.shape[1]
    tn = _tile(n, 512, LANES)
    return pl.pallas_call(
        _ada_kernel, grid=(n // tn,),
        in_specs=[pl.BlockSpec((m, k), lambda j: (0, 0)),
                  pl.BlockSpec((k, tn), lambda j: (0, j)),
                  pl.BlockSpec((1, tn), lambda j: (0, j))],
        out_specs=pl.BlockSpec((m, tn), lambda j: (0, j)),
        out_shape=jax.ShapeDtypeStruct((m, n), F32),
        compiler_params=_cparams(("arbitrary",)), name="ada_proj")(c, w, b.reshape(1, n))


def _mm_kernel(a_ref, w_ref, o_ref):
    o_ref[...] = _dot(a_ref[...], w_ref[...]).astype(o_ref.dtype)


def _matmul(a, w, tm, tn, out_dtype, name):
    m, k = a.shape
    n = w.shape[1]
    return pl.pallas_call(
        _mm_kernel, grid=(n // tn, m // tm),
        in_specs=[pl.BlockSpec((tm, k), lambda j, i: (i, 0)),
                  pl.BlockSpec((k, tn), lambda j, i: (0, j))],
        out_specs=pl.BlockSpec((tm, tn), lambda j, i: (i, j)),
        out_shape=jax.ShapeDtypeStruct((m, n), out_dtype),
        compiler_params=_cparams(("arbitrary", "arbitrary")), name=name)(a, w)


def _xp_spec(d, nch):
    return pl.BlockSpec((None, CHUNK, d), lambda s, b: (b, jnp.minimum(s, nch - 1), 0))


def _xs_spec(d, nch):
    return pl.BlockSpec((None, CHUNK, d), lambda s, b: (jnp.where(s < nch, 0, b), 0, 0))


def _mod_spec(d, nch):
    return pl.BlockSpec((None, None, N_MOD, d), lambda s, b: (jnp.where(s < nch, 0, 1), b, 0, 0))


def _row_spec(width, nb, col=0):
    return pl.BlockSpec((CHUNK, width), lambda s, b: (s * nb + b, col))


def _prenorm_kernel(xp_ref, xs_ref, mod_ref, g_ref, h_ref, *, nch):
    s = pl.program_id(0)
    x = jnp.where(s < nch, xp_ref[...], xs_ref[...])
    shift, scale = mod_ref[0:1, :], mod_ref[1:2, :]
    h_ref[...] = (_rms(x, g_ref[...]) * (1.0 + scale) + shift).astype(h_ref.dtype)


def _prenorm(xp, xs, mod, g, nch):
    nb, _, d = xp.shape
    t = (nch + 1) * nb * CHUNK
    return pl.pallas_call(
        functools.partial(_prenorm_kernel, nch=nch), grid=(nch + 1, nb),
        in_specs=[_xp_spec(d, nch), _xs_spec(d, nch), _mod_spec(d, nch),
                  pl.BlockSpec((1, d), lambda s, b: (0, 0))],
        out_specs=_row_spec(d, nb),
        out_shape=jax.ShapeDtypeStruct((t, d), BF16),
        compiler_params=_cparams(("arbitrary", "arbitrary")), name="prenorm")(xp, xs, mod, g)


def _to_token_tiles(dst_ref, rows):
    n, w = rows.shape
    nt = w // LANES
    for c in range(nt):
        dst_ref[pl.ds(c, n, stride=nt), :] = rows[:, c * LANES:(c + 1) * LANES]


def _slot_rows(nt):
    return nt + 1


def _from_token_tiles(src_ref, n, nt, c):
    return src_ref[pl.ds(c, n, stride=_slot_rows(nt)), :]


def _midnorm_kernel(xp_ref, xs_ref, y_ref, mod_ref, g1_ref, g2_ref, x1_ref, h2_ref, h2t_ref, *, nch):
    s = pl.program_id(0)
    x = jnp.where(s < nch, xp_ref[...], xs_ref[...])
    gate1, shift2, scale2 = mod_ref[2:3, :], mod_ref[3:4, :], mod_ref[4:5, :]
    x1 = x + gate1 * _rms(y_ref[...], g1_ref[...])
    x1_ref[...] = x1
    h2 = _rms(x1, g2_ref[...]) * (1.0 + scale2) + shift2
    h2_ref[...] = h2
    _to_token_tiles(h2t_ref, h2)


def _midnorm(xp, xs, y, mod, g1, g2, nch):
    nb, _, d = xp.shape
    t = (nch + 1) * nb * CHUNK
    vec = pl.BlockSpec((1, d), lambda s, b: (0, 0))
    nt = d // LANES
    tiles = pl.BlockSpec((CHUNK * nt, LANES), lambda s, b: (s * nb + b, 0))
    return pl.pallas_call(
        functools.partial(_midnorm_kernel, nch=nch), grid=(nch + 1, nb),
        in_specs=[_xp_spec(d, nch), _xs_spec(d, nch), _row_spec(d, nb), _mod_spec(d, nch), vec, vec],
        out_specs=[_row_spec(d, nb), _row_spec(d, nb), tiles],
        out_shape=[jax.ShapeDtypeStruct((t, d), F32), jax.ShapeDtypeStruct((t, d), F32),
                   jax.ShapeDtypeStruct((t * nt, LANES), F32)],
        compiler_params=_cparams(("arbitrary", "arbitrary")), name="midnorm")(xp, xs, y, mod, g1, g2)


def _prompt_state_spec(block, nch, nb):
    tail = (0,) * (len(block) - 1)
    return pl.BlockSpec(block, lambda s, b: (jnp.where(s == nch - 1, b, jnp.where(s < nch - 1, 0, nb - 1)),) + tail)


def _sample_state_spec(block, nch):
    tail = (0,) * (len(block) - 1)
    return pl.BlockSpec(block, lambda s, b: (jnp.where(s == nch, b, 0),) + tail)


def _tri(n, strict=False, dtype=F32):
    i = lax.broadcasted_iota(I32, (n, n), 0)
    j = lax.broadcasted_iota(I32, (n, n), 1)
    return ((i > j) if strict else (i >= j)).astype(dtype)


GDN_HEAD_GROUP = 16
assert 2 * CHUNK == LANES


def _gdn_kernel(qkv_ref, z_ref, ab_ref, convw_ref, conv0_ref, alog_ref, dtb_ref, gn_ref, s0_ref,
                oa_ref, convp_ref, convs_ref, gdnp_ref, gdns_ref,
                tails, state, full, act, qn, kn, *, nch, hq, hv, dh, group):
    s = pl.program_id(0)
    b = pl.program_id(1)
    kd = hq * dh
    rep = hv // hq

    @pl.when(s == 0)
    def _():
        tails[b] = jnp.zeros(tails.shape[1:], F32)
        state[b] = jnp.zeros(state.shape[1:], F32)

    @pl.when(s == nch)
    def _():
        tails[b] = conv0_ref[...]
        state[b] = s0_ref[...]

    full[0:SUBLANES, :] = tails[b]
    full[SUBLANES:SUBLANES + CHUNK, :] = qkv_ref[...]
    base = SUBLANES - (CONV_WIDTH - 1)
    acc = full[base:base + CHUNK, :] * convw_ref[0:1, :]
    for i in range(1, CONV_WIDTH):
        acc = acc + full[base + i:base + i + CHUNK, :] * convw_ref[i:i + 1, :]
    act[...] = _silu(acc)
    new_tail = full[CHUNK:CHUNK + SUBLANES, :]
    tails[b] = new_tail

    for h in range(hq):
        q = act[:, h * dh:(h + 1) * dh]
        k = act[:, kd + h * dh:kd + (h + 1) * dh]
        qn[:, h * dh:(h + 1) * dh] = q * lax.rsqrt(jnp.sum(q * q, axis=-1, keepdims=True) + EPS) * dh ** -0.5
        kn[:, h * dh:(h + 1) * dh] = k * lax.rsqrt(jnp.sum(k * k, axis=-1, keepdims=True) + EPS)

    ab = ab_ref[...]
    g_all = -jnp.exp(alog_ref[...]) * _softplus(ab + dtb_ref[...])
    beta_all = jax.nn.sigmoid(ab)
    gc_all = _dot(_tri(CHUNK), g_all, HIGHEST)
    gc_t = jnp.concatenate([gc_all, gc_all], axis=0).T
    eg_all = jnp.exp(gc_all)
    g_last = gc_all[CHUNK - 1:CHUNK, :]
    ekd_all = jnp.exp(g_last - gc_all)
    edec_all = jnp.exp(g_last)

    ii = lax.broadcasted_iota(I32, (CHUNK, LANES), 0)
    lane = lax.broadcasted_iota(I32, (CHUNK, LANES), 1)
    first = lane < CHUNK
    jj = jnp.where(first, lane, lane - CHUNK)
    incl = ii >= jj
    strict = ii > jj
    eye = (ii == jj).astype(F32)
    zeros_bf = jnp.zeros((CHUNK, LANES), BF16)

    def halves(x, with_rhs=True):
        hi = x.astype(BF16)
        rest = x - hi.astype(F32)
        rhs = None
        if with_rhs:
            rhs = jnp.concatenate([jnp.where(first, x, rest).astype(BF16), zeros_bf], axis=0)
        return hi, rest.astype(BF16), rhs

    def fold(r):
        sm = r[:CHUNK] + r[CHUNK:]
        return sm + pltpu.roll(sm, CHUNK, axis=1)

    for h0 in range(0, hv, group):
        heads = range(h0, min(h0 + group, hv))
        kh, qh, kb, kcat, dec, nn, tm = {}, {}, {}, {}, {}, {}, {}
        for h in heads:
            kh[h] = kn[:, (h // rep) * dh:(h // rep + 1) * dh]
            qh[h] = qn[:, (h // rep) * dh:(h // rep + 1) * dh]
            kcat[h] = jnp.concatenate([kh[h], kh[h]], axis=0).astype(BF16)
            beta = beta_all[:, hv + h:hv + h + 1]
            kb[h] = kh[h] * beta
            diff = gc_all[:, h:h + 1] - gc_t[h:h + 1, :]
            dec[h] = jnp.exp(jnp.where(incl, diff, NEG_BIG))
        for h in heads:
            a = _dot_nt(kb[h].astype(BF16), kcat[h])
            nn[h] = jnp.where(strict, -(a * dec[h]), 0.0)
            tm[h] = eye + nn[h]
        for h in heads:
            p_hi, p_lo, rhs = halves(nn[h])
            nn[h] = fold(_dot(jnp.concatenate([p_hi, p_lo], axis=0), rhs))
        p = 4
        while p < CHUNK:
            for h in heads:
                p_hi, p_lo, rhs = halves(nn[h])
                t_hi, t_lo, _ = halves(tm[h], with_rhs=False)
                r = _dot(jnp.concatenate([p_hi, p_lo, t_hi, t_lo], axis=0), rhs)
                nn[h] = fold(r[:2 * CHUNK])
                tm[h] = tm[h] + fold(r[2 * CHUNK:])
            p *= 2
        for h in heads:
            _, _, rhs = halves(nn[h])
            t_hi, t_lo, _ = halves(tm[h], with_rhs=False)
            tm[h] = tm[h] + fold(_dot(jnp.concatenate([t_hi, t_lo], axis=0), rhs))
        u, w, attn = {}, {}, {}
        for h in heads:
            beta = beta_all[:, hv + h:hv + h + 1]
            vh = act[:, 2 * kd + h * dh:2 * kd + (h + 1) * dh]
            rhs = jnp.concatenate([vh * beta, kb[h] * eg_all[:, h:h + 1]], axis=-1).astype(BF16)
            sol = _dot(tm[h][:, :CHUNK].astype(BF16), rhs)
            u[h], w[h] = sol[:, :dh], sol[:, dh:]
            attn[h] = (_dot_nt(qh[h].astype(BF16), kcat[h]) * dec[h])[:, :CHUNK].astype(BF16)
        vnew, o = {}, {}
        for h in heads:
            lhs = jnp.concatenate([w[h], qh[h] * eg_all[:, h:h + 1]], axis=0).astype(BF16)
            ws_qs = _dot(lhs, state[b, h].astype(BF16))
            vnew[h] = u[h] - ws_qs[:CHUNK]
            o[h] = ws_qs[CHUNK:] + _dot(attn[h], vnew[h].astype(BF16))
        for h in heads:
            kdec_t = (kh[h] * ekd_all[:, h:h + 1]).T.astype(BF16)
            state[b, h] = state[b, h] * edec_all[0:1, h:h + 1] + _dot(kdec_t, vnew[h].astype(BF16))
        for h in heads:
            zh = z_ref[:, h * dh:(h + 1) * dh]
            oa_ref[:, h * dh:(h + 1) * dh] = (_rms(o[h], gn_ref[...]) * _silu(zh)).astype(oa_ref.dtype)

    @pl.when(s == nch - 1)
    def _():
        convp_ref[...] = new_tail
        gdnp_ref[...] = state[b]

    @pl.when(s == nch)
    def _():
        convs_ref[...] = new_tail
        gdns_ref[...] = state[b]


def _gdn(p_main, p_small, conv_w, conv0, a_log, dt_bias, gdn_norm, s0, *, nch, nb, hq, hv, dh, z_col):
    t = p_main.shape[0]
    cd = conv_w.shape[1]
    vd = hv * dh
    kd = hq * dh
    vec = lambda n: pl.BlockSpec((1, n), lambda s, b: (0, 0))
    st_block = (None, hv, dh, dh)
    cv_block = (None, SUBLANES, cd)
    return pl.pallas_call(
        functools.partial(_gdn_kernel, nch=nch, hq=hq, hv=hv, dh=dh, group=GDN_HEAD_GROUP), grid=(nch + 1, nb),
        in_specs=[_row_spec(cd, nb), _row_spec(vd, nb, z_col), _row_spec(LANES, nb),
                  pl.BlockSpec((CONV_WIDTH, cd), lambda s, b: (0, 0)),
                  _sample_state_spec(cv_block, nch), vec(LANES), vec(LANES), vec(dh),
                  _sample_state_spec(st_block, nch)],
        out_specs=[_row_spec(vd, nb),
                   _prompt_state_spec(cv_block, nch, nb), _sample_state_spec(cv_block, nch),
                   _prompt_state_spec(st_block, nch, nb), _sample_state_spec(st_block, nch)],
        out_shape=[jax.ShapeDtypeStruct((t, vd), BF16),
                   jax.ShapeDtypeStruct((nb, SUBLANES, cd), F32), jax.ShapeDtypeStruct((nb, SUBLANES, cd), F32),
                   jax.ShapeDtypeStruct((nb, hv, dh, dh), F32), jax.ShapeDtypeStruct((nb, hv, dh, dh), F32)],
        scratch_shapes=[pltpu.VMEM((nb, SUBLANES, cd), F32), pltpu.VMEM((nb, hv, dh, dh), F32),
                        pltpu.VMEM((SUBLANES + CHUNK, cd), F32), pltpu.VMEM((CHUNK, cd), F32),
                        pltpu.VMEM((CHUNK, kd), F32), pltpu.VMEM((CHUNK, kd), F32)],
        compiler_params=_cparams(("arbitrary", "arbitrary")), name="gdn")(
            p_main, p_main, p_small, conv_w, conv0, a_log, dt_bias, gdn_norm, s0)


def _gla_kernel(q_ref, k_ref, v_ref, r_ref, ab_ref, w2_ref, gb_ref, gn_ref, s0_ref,
                ob_ref, glap_ref, glas_ref, state, *, nch, nh, hk, hvd):
    s = pl.program_id(0)
    b = pl.program_id(1)

    @pl.when(s == 0)
    def _():
        state[b] = jnp.zeros(state.shape[1:], F32)

    @pl.when(s == nch)
    def _():
        state[b] = s0_ref[...]

    pre = _dot(ab_ref[...], w2_ref[...], HIGHEST) + gb_ref[...]
    log_sig = jnp.minimum(pre, 0.0) - jnp.log1p(jnp.exp(-jnp.abs(pre)))
    gk = jnp.maximum(log_sig / GLA_GATE_TEMP, GLA_LOG_DECAY_FLOOR)
    bc = _dot(_tri(CHUNK), gk, HIGHEST)
    b_last = bc[CHUNK - 1:CHUNK, :]
    qd = (q_ref[...] * hk ** -0.5 * jnp.exp(bc)).astype(BF16)
    ki = (k_ref[...] * jnp.exp(-bc)).astype(BF16)
    kdec = k_ref[...] * jnp.exp(b_last - bc)
    edec = jnp.exp(b_last)
    incl = _tri(CHUNK) > 0.0

    for h in range(nh):
        ks = slice(h * hk, (h + 1) * hk)
        vs = slice(h * hvd, (h + 1) * hvd)
        vh = v_ref[:, vs].astype(BF16)
        attn = jnp.where(incl, _dot_nt(qd[:, ks], ki[:, ks]), 0.0).astype(BF16)
        o = _dot(attn, vh) + _dot(qd[:, ks], state[b, h].astype(BF16))
        edec_col = jnp.broadcast_to(edec[:, ks], (SUBLANES, hk)).T[:, 0:1]
        state[b, h] = state[b, h] * edec_col + _dot(kdec[:, ks].T.astype(BF16), vh)
        ob_ref[:, vs] = (_rms(o, gn_ref[...]) * _silu(r_ref[:, vs])).astype(ob_ref.dtype)

    @pl.when(s == nch - 1)
    def _():
        glap_ref[...] = state[b]

    @pl.when(s == nch)
    def _():
        glas_ref[...] = state[b]


def _gla(p_main, p_small, w2_pad, gate_b, gla_norm, s0, *, nch, nb, nh, hk, hvd, cols):
    t = p_main.shape[0]
    kl, vl = nh * hk, nh * hvd
    st_block = (None, nh, hk, hvd)
    q_col, k_col, v_col, r_col = cols
    return pl.pallas_call(
        functools.partial(_gla_kernel, nch=nch, nh=nh, hk=hk, hvd=hvd), grid=(nch + 1, nb),
        in_specs=[_row_spec(kl, nb, q_col), _row_spec(kl, nb, k_col), _row_spec(vl, nb, v_col),
                  _row_spec(vl, nb, r_col), _row_spec(LANES, nb),
                  pl.BlockSpec((LANES, kl), lambda s, b: (0, 0)),
                  pl.BlockSpec((1, kl), lambda s, b: (0, 0)),
                  pl.BlockSpec((1, hvd), lambda s, b: (0, 0)),
                  _sample_state_spec(st_block, nch)],
        out_specs=[_row_spec(vl, nb), _prompt_state_spec(st_block, nch, nb), _sample_state_spec(st_block, nch)],
        out_shape=[jax.ShapeDtypeStruct((t, vl), BF16),
                   jax.ShapeDtypeStruct((nb, nh, hk, hvd), F32), jax.ShapeDtypeStruct((nb, nh, hk, hvd), F32)],
        scratch_shapes=[pltpu.VMEM((nb, nh, hk, hvd), F32)],
        compiler_params=_cparams(("arbitrary", "arbitrary")), name="gla")(
            p_main, p_main, p_main, p_main, p_small, w2_pad, gate_b, gla_norm, s0)


def _merge_kernel(oa_ref, ob_ref, wa_ref, wb_ref, g0_ref, g1_ref, o_ref):
    ya = _dot(oa_ref[...], wa_ref[...])
    yb = _dot(ob_ref[...], wb_ref[...])
    o_ref[...] = (jax.nn.sigmoid(g0_ref[...]) * ya + jax.nn.sigmoid(g1_ref[...]) * yb).astype(o_ref.dtype)


def _merge(oa, ob, wa, wb, p_main, br_off, tm, tn):
    t, ka = oa.shape
    kb = ob.shape[1]
    d = wa.shape[1]
    c0, c1 = br_off // tn, (br_off + d) // tn
    return pl.pallas_call(
        _merge_kernel, grid=(d // tn, t // tm),
        in_specs=[pl.BlockSpec((tm, ka), lambda j, i: (i, 0)), pl.BlockSpec((tm, kb), lambda j, i: (i, 0)),
                  pl.BlockSpec((ka, tn), lambda j, i: (0, j)), pl.BlockSpec((kb, tn), lambda j, i: (0, j)),
                  pl.BlockSpec((tm, tn), lambda j, i: (i, c0 + j)), pl.BlockSpec((tm, tn), lambda j, i: (i, c1 + j))],
        out_specs=pl.BlockSpec((tm, tn), lambda j, i: (i, j)),
        out_shape=jax.ShapeDtypeStruct((t, d), BF16),
        compiler_params=_cparams(("arbitrary", "arbitrary")), name="merge")(oa, ob, wa, wb, p_main, p_main)


def _router_kernel(h_ref, w_ref, b_ref, idx_ref, wgt_ref, rank_ref, cnt_ref, run, *, tm):
    @pl.when(pl.program_id(0) == 0)
    def _():
        run[...] = jnp.zeros(run.shape, F32)

    logits = _dot(h_ref[...], w_ref[...], HIGHEST) + b_ref[...]
    lane = lax.broadcasted_iota(I32, logits.shape, 1).astype(F32)
    cur = logits
    tops, sels, idxs = [], [], []
    for _ in range(TOP_K):
        m = jnp.max(cur, axis=-1, keepdims=True)
        idx = jnp.min(jnp.where(cur == m, lane, float(LANES)), axis=-1, keepdims=True)
        sel = lane == idx
        tops.append(m)
        sels.append(sel)
        idxs.append(idx)
        cur = jnp.where(sel, 2.0 * NEG_BIG, cur)
    exps = [jnp.exp(m - tops[0]) for m in tops]
    denom = exps[0]
    for e in exps[1:]:
        denom = denom + e
    onehot = sels[0].astype(F32)
    for sel in sels[1:]:
        onehot = onehot + sel.astype(F32)
    prefix = _dot(_tri(tm, strict=True, dtype=BF16), onehot.astype(BF16)) + run[0:1, :]
    idx_out = jnp.zeros(logits.shape, F32)
    wgt_out = jnp.zeros(logits.shape, F32)
    rank_out = jnp.zeros(logits.shape, F32)
    for k in range(TOP_K):
        rank = jnp.sum(jnp.where(sels[k], prefix, 0.0), axis=-1, keepdims=True)
        idx_out = jnp.where(lane == float(k), idxs[k], idx_out)
        wgt_out = jnp.where(lane == float(k), exps[k] / denom, wgt_out)
        rank_out = jnp.where(lane == float(k), rank, rank_out)
    idx_ref[...] = idx_out.astype(I32)
    wgt_ref[...] = wgt_out
    rank_ref[...] = rank_out.astype(I32)
    total = run[...] + jnp.sum(onehot, axis=0, keepdims=True)
    run[...] = total
    cnt_ref[...] = total.astype(I32)


def _router(h2, w_pad, b_pad, tm):
    t, d = h2.shape
    row = pl.BlockSpec((tm, LANES), lambda i: (i, 0))
    return pl.pallas_call(
        functools.partial(_router_kernel, tm=tm), grid=(t // tm,),
        in_specs=[pl.BlockSpec((tm, d), lambda i: (i, 0)), pl.BlockSpec((d, LANES), lambda i: (0, 0)),
                  pl.BlockSpec((1, LANES), lambda i: (0, 0))],
        out_specs=[row, row, row, pl.BlockSpec((SUBLANES, LANES), lambda i: (0, 0))],
        out_shape=[jax.ShapeDtypeStruct((t, LANES), I32), jax.ShapeDtypeStruct((t, LANES), F32),
                   jax.ShapeDtypeStruct((t, LANES), I32), jax.ShapeDtypeStruct((SUBLANES, LANES), I32)],
        scratch_shapes=[pltpu.VMEM((SUBLANES, LANES), F32)],
        compiler_params=_cparams(("arbitrary",)), name="router")(h2, w_pad, b_pad)


def _row_copy(src_hbm, dst, row, slot, nt, sem):
    return pltpu.make_async_copy(src_hbm.at[row], dst.at[pl.ds(slot * _slot_rows(nt), nt)], sem)


GATHER_UNROLL = 8


def _dispatch_kernel(nu_ref, tok_ref, src_hbm, o_ref, buf, sem, *, tg, nt):
    i = pl.program_id(0)

    @pl.when(i < nu_ref[0])
    def _():
        def issue(r, c):
            _row_copy(src_hbm, buf, tok_ref[r], r, nt, sem).start()
            return c

        def drain(r, c):
            _row_copy(src_hbm, buf, tok_ref[r], r, nt, sem).wait()
            return c

        lax.fori_loop(0, tg, issue, 0, unroll=GATHER_UNROLL)
        lax.fori_loop(0, tg, drain, 0, unroll=GATHER_UNROLL)
        for c in range(nt):
            o_ref[:, c * LANES:(c + 1) * LANES] = _from_token_tiles(buf, tg, nt, c).astype(o_ref.dtype)

    @pl.when(i >= nu_ref[0])
    def _():
        o_ref[...] = jnp.zeros(o_ref.shape, o_ref.dtype)


def _dispatch(row_token, n_used, h2_tiles, tg):
    r = row_token.shape[0]
    nt = h2_tiles.shape[1]
    d = nt * LANES
    return pl.pallas_call(
        functools.partial(_dispatch_kernel, tg=tg, nt=nt),
        grid_spec=pltpu.PrefetchScalarGridSpec(
            num_scalar_prefetch=1, grid=(r // tg,),
            in_specs=[pl.BlockSpec((tg,), lambda i, nu: (i,), memory_space=pltpu.SMEM),
                      pl.BlockSpec(memory_space=pl.ANY)],
            out_specs=pl.BlockSpec((tg, d), lambda i, nu: (i, 0)),
            scratch_shapes=[pltpu.VMEM((tg * _slot_rows(nt), LANES), F32), pltpu.SemaphoreType.DMA(())]),
        out_shape=jax.ShapeDtypeStruct((r, d), BF16),
        compiler_params=_cparams(("arbitrary",), row_gather=True), name="moe_dispatch")(
            n_used, row_token, h2_tiles)


def _expert_changed(be_ref, i):
    return jnp.logical_or(i == 0, be_ref[i] != be_ref[jnp.maximum(i - 1, 0)])


def _for_filled_rows(i, nu_ref, half_ref, o_ref, compute):
    tm = o_ref.shape[0]
    used = i < nu_ref[0]
    half_full = half_ref[i] == 1

    @pl.when(jnp.logical_and(used, jnp.logical_not(half_full)))
    def _():
        compute(slice(0, tm))

    @pl.when(jnp.logical_and(used, half_full))
    def _():
        compute(slice(0, tm // 2))
        o_ref[tm // 2:] = jnp.zeros((tm - tm // 2,) + o_ref.shape[1:], o_ref.dtype)

    @pl.when(jnp.logical_not(used))
    def _():
        o_ref[...] = jnp.zeros(o_ref.shape, o_ref.dtype)


def _gate_up_kernel(be_ref, nu_ref, half_ref, x_ref, wg_ref, wl_ref, bg_ref, bl_ref, o_ref, wg_bf, wl_bf):
    i = pl.program_id(1)

    @pl.when(jnp.logical_and(i < nu_ref[0], _expert_changed(be_ref, i)))
    def _():
        wg_bf[...] = wg_ref[...].astype(BF16)
        wl_bf[...] = wl_ref[...].astype(BF16)

    def compute(rows):
        x = x_ref[rows, :]
        glu = jnp.minimum(_dot(x, wg_bf[...]) + bg_ref[...], SWIGLU_LIMIT)
        lin = jnp.clip(_dot(x, wl_bf[...]) + bl_ref[...], -SWIGLU_LIMIT, SWIGLU_LIMIT)
        o_ref[rows, :] = (glu * jax.nn.sigmoid(SWIGLU_ALPHA * glu) * (lin + 1.0)).astype(o_ref.dtype)

    _for_filled_rows(i, nu_ref, half_ref, o_ref, compute)


def _down_kernel(be_ref, nu_ref, half_ref, a_ref, w_ref, b_ref, o_ref, w_bf):
    i = pl.program_id(1)

    @pl.when(jnp.logical_and(i < nu_ref[0], _expert_changed(be_ref, i)))
    def _():
        w_bf[...] = w_ref[...].astype(BF16)

    def compute(rows):
        o_ref[rows, :] = _dot(a_ref[rows, :], w_bf[...]) + b_ref[...]

    _for_filled_rows(i, nu_ref, half_ref, o_ref, compute)


def _moe_experts(xs, block_expert, n_used, block_half, w_gate_up, b_gate_up, w_down, b_down, tm):
    r, d = xs.shape
    ne, _, two_de = w_gate_up.shape
    de = two_de // 2
    nb = r // tm
    tn = _tile(de, 512, LANES)
    nj = de // tn

    def blk(i, nu):
        return jnp.maximum(jnp.minimum(i, nu[0] - 1), 0)

    act = pl.pallas_call(
        _gate_up_kernel,
        grid_spec=pltpu.PrefetchScalarGridSpec(
            num_scalar_prefetch=3, grid=(nj, nb),
            in_specs=[pl.BlockSpec((tm, d), lambda j, i, be, nu, hf: (blk(i, nu), 0)),
                      pl.BlockSpec((None, d, tn), lambda j, i, be, nu, hf: (be[blk(i, nu)], 0, j)),
                      pl.BlockSpec((None, d, tn), lambda j, i, be, nu, hf: (be[blk(i, nu)], 0, nj + j)),
                      pl.BlockSpec((None, 1, tn), lambda j, i, be, nu, hf: (be[blk(i, nu)], 0, j)),
                      pl.BlockSpec((None, 1, tn), lambda j, i, be, nu, hf: (be[blk(i, nu)], 0, nj + j))],
            out_specs=pl.BlockSpec((tm, tn), lambda j, i, be, nu, hf: (i, j)),
            scratch_shapes=[pltpu.VMEM((d, tn), BF16), pltpu.VMEM((d, tn), BF16)]),
        out_shape=jax.ShapeDtypeStruct((r, de), BF16),
        compiler_params=_cparams(("arbitrary", "arbitrary")), name="moe_gate_up")(
            block_expert, n_used, block_half, xs, w_gate_up, w_gate_up,
            b_gate_up.reshape(ne, 1, two_de), b_gate_up.reshape(ne, 1, two_de))

    dm = w_down.shape[2]
    tn = _tile(dm, 1024, LANES)
    njd = dm // tn
    return pl.pallas_call(
        _down_kernel,
        grid_spec=pltpu.PrefetchScalarGridSpec(
            num_scalar_prefetch=3, grid=(njd, nb),
            in_specs=[pl.BlockSpec((tm, de), lambda j, i, be, nu, hf: (blk(i, nu), 0)),
                      pl.BlockSpec((None, de, tn), lambda j, i, be, nu, hf: (be[blk(i, nu)], 0, j)),
                      pl.BlockSpec((None, 1, tn), lambda j, i, be, nu, hf: (be[blk(i, nu)], 0, j))],
            out_specs=pl.BlockSpec((tm, tn), lambda j, i, be, nu, hf: (i, j)),
            scratch_shapes=[pltpu.VMEM((de, tn), BF16)]),
        out_shape=jax.ShapeDtypeStruct((r, dm), F32),
        compiler_params=_cparams(("arbitrary", "arbitrary")), name="moe_down")(
            block_expert, n_used, block_half, act, w_down, b_down.reshape(ne, 1, dm))


def _strided_row_copy(src_hbm, dst, row, slot, sem):
    return pltpu.make_async_copy(src_hbm.at[pl.ds(row, 1)], dst.at[pl.ds(slot, 1)], sem)


def _combine_kernel(dest_ref, wgt_ref, x1_ref, mod_ref, g_ref, yb_hbm, op_ref, os_ref, buf, sem, *, nch):
    s = pl.program_id(0)

    def issue(n, c):
        t, k = n // TOP_K, n % TOP_K
        _strided_row_copy(yb_hbm, buf.at[k], dest_ref[n], t, sem).start()
        return c

    def drain(n, c):
        t, k = n // TOP_K, n % TOP_K
        _strided_row_copy(yb_hbm, buf.at[k], dest_ref[n], t, sem).wait()
        return c

    lax.fori_loop(0, CHUNK * TOP_K, issue, 0, unroll=GATHER_UNROLL)
    lax.fori_loop(0, CHUNK * TOP_K, drain, 0, unroll=GATHER_UNROLL)
    wgt = wgt_ref[...]
    y = wgt[:, 0:1] * buf[0]
    for k in range(1, TOP_K):
        y = y + wgt[:, k:k + 1] * buf[k]
    out = x1_ref[...] + mod_ref[5:6, :] * _rms(y, g_ref[...])

    @pl.when(s < nch)
    def _():
        op_ref[...] = out

    @pl.when(s == nch)
    def _():
        os_ref[...] = out


def _combine(dest, wgt, x1, mod, g, yb, nch, nb, seq):
    t, d = x1.shape
    return pl.pallas_call(
        functools.partial(_combine_kernel, nch=nch), grid=(nch + 1, nb),
        in_specs=[pl.BlockSpec((CHUNK * TOP_K,), lambda s, b: (s * nb + b,), memory_space=pltpu.SMEM),
                  _row_spec(LANES, nb), _row_spec(d, nb), _mod_spec(d, nch),
                  pl.BlockSpec((1, d), lambda s, b: (0, 0)),
                  pl.BlockSpec(memory_space=pl.ANY)],
        out_specs=[pl.BlockSpec((None, CHUNK, d),
                                lambda s, b: (jnp.where(s < nch, b, nb - 1), jnp.minimum(s, nch - 1), 0)),
                   pl.BlockSpec((None, CHUNK, d), lambda s, b: (jnp.where(s == nch, b, 0), 0, 0))],
        out_shape=[jax.ShapeDtypeStruct((nb, seq, d), F32), jax.ShapeDtypeStruct((nb, CHUNK, d), F32)],
        scratch_shapes=[pltpu.VMEM((TOP_K, CHUNK, d), F32), pltpu.SemaphoreType.DMA(())],
        compiler_params=_cparams(("arbitrary", "arbitrary"), row_gather=True), name="moe_combine")(
            dest, wgt, x1, mod, g, yb)


def _layer(x_prompt, x_sample, c_prompt, c_sample, conv_s0, gdn_s0, gla_s0,
           w_ada, b_ada, n_mix_pre, n_mix_post, n_ffn_pre, n_ffn_post, w_in, conv_w, a_log, dt_bias,
           gdn_norm, gla_w2, gla_b, gla_norm, w_branch, w_out, w_router, b_router,
           w_gate_up, b_gate_up, w_down, b_down):
    nb, seq, d = x_prompt.shape
    assert x_sample.shape == (nb, CHUNK, d) and seq % CHUNK == 0
    nch = seq // CHUNK
    t = (nch + 1) * nb * CHUNK
    hv, dh = a_log.shape[0], gdn_norm.shape[0]
    cd = conv_w.shape[1]
    vd = hv * dh
    kd = (cd - vd) // 2
    hq = kd // dh
    nh, hk, hvd = gla_s0.shape[1:]
    kl, vl = nh * hk, nh * hvd
    rank = gla_w2.shape[0]
    ne = w_router.shape[1]
    assert 2 * hv + rank <= LANES and ne <= LANES

    c_all = jnp.concatenate([c_prompt, c_sample], axis=0)
    mod = _ada_proj(c_all, w_ada, b_ada).reshape(2, nb, N_MOD, d)

    sizes = (cd, hv, hv, vd, kl, kl, vl, rank, vl, 2 * d)
    offs = [0]
    for sz in sizes:
        offs.append(offs[-1] + sz)
    seg = lambda i: w_in[:, offs[i]:offs[i + 1]]
    main_ids = (0, 3, 4, 5, 6, 8, 9)
    w_main = jnp.concatenate([seg(i) for i in main_ids], axis=1).astype(BF16)
    n_small = 2 * hv + rank
    w_small = jnp.concatenate([seg(1), seg(2), seg(7), jnp.zeros((d, LANES - n_small), F32)], axis=1).astype(BF16)
    moff = [0]
    for i in main_ids:
        moff.append(moff[-1] + sizes[i])
    z_off, lq_off, lk_off, lv_off, lr_off, br_off = moff[1:7]
    assert z_off % vd == 0 and lq_off % kl == 0 and lk_off % kl == 0 and lv_off % vl == 0 and lr_off % vl == 0

    h1 = _prenorm(x_prompt, x_sample, mod, n_mix_pre.reshape(1, d), nch)
    tm = _tile(t, 512, CHUNK)
    p_main = _matmul(h1, w_main, tm, _tile(w_main.shape[1], 1024, LANES), F32, "in_proj")
    p_small = _matmul(h1, w_small, tm, LANES, F32, "in_proj_small")

    pad_lanes = lambda v: jnp.pad(v.reshape(1, -1), ((0, 0), (0, LANES - v.shape[0])))
    conv0 = jnp.pad(conv_s0, ((0, 0), (SUBLANES - (CONV_WIDTH - 1), 0), (0, 0)))
    oa, convp, convs, gdnp, gdns = _gdn(
        p_main, p_small, conv_w, conv0, pad_lanes(a_log), pad_lanes(dt_bias), gdn_norm.reshape(1, dh), gdn_s0,
        nch=nch, nb=nb, hq=hq, hv=hv, dh=dh, z_col=z_off // vd)
    w2_pad = jnp.zeros((LANES, kl), F32).at[2 * hv:2 * hv + rank].set(gla_w2)
    ob, glap, glas = _gla(
        p_main, p_small, w2_pad, gla_b.reshape(1, kl), gla_norm.reshape(1, hvd), gla_s0,
        nch=nch, nb=nb, nh=nh, hk=hk, hvd=hvd,
        cols=(lq_off // kl, lk_off // kl, lv_off // vl, lr_off // vl))

    tn = _tile(d, 1024, LANES)
    assert br_off % tn == 0
    ym = _merge(oa, ob, w_branch[:vd].astype(BF16), w_branch[vd:].astype(BF16), p_main, br_off, tm, tn)
    y2 = _matmul(ym, w_out.astype(BF16), tm, tn, F32, "out_proj")
    x1, h2, h2_tiles = _midnorm(x_prompt, x_sample, y2, mod, n_mix_post.reshape(1, d), n_ffn_pre.reshape(1, d), nch)

    w_r = jnp.pad(w_router, ((0, 0), (0, LANES - ne)))
    b_r = jnp.pad(b_router.reshape(1, ne), ((0, 0), (0, LANES - ne)), constant_values=NEG_BIG)
    idx, wgt, rnk, cnt = _router(h2, w_r, b_r, tm)
    tmm = _tile(t * TOP_K, 512, CHUNK)
    counts = cnt[0, :ne]
    padded = (counts + tmm - 1) // tmm * tmm
    pad_end = jnp.cumsum(padded)
    pad_start = pad_end - padded
    dest = (pad_start[idx[:, :TOP_K]] + rnk[:, :TOP_K]).astype(I32)
    n_blocks = t * TOP_K // tmm + ne
    n_rows = n_blocks * tmm
    token_of_pair = jnp.repeat(jnp.arange(t, dtype=I32), TOP_K)
    row_token = jnp.zeros((n_rows,), I32).at[dest.reshape(-1)].set(token_of_pair)
    block_start = jnp.arange(n_blocks, dtype=I32) * tmm
    block_expert = jnp.minimum(jnp.sum(block_start[:, None] >= pad_end[None, :], axis=1), ne - 1).astype(I32)
    n_used = (pad_end[-1:] // tmm).astype(I32)
    filled = counts[block_expert] - (block_start - pad_start[block_expert])
    block_half = (filled <= tmm // 2).astype(I32)

    xs = _dispatch(row_token, n_used, h2_tiles.reshape(t, d // LANES, LANES), tmm)
    yb = _moe_experts(xs, block_expert, n_used, block_half, w_gate_up, b_gate_up, w_down, b_down, tmm)
    y_prompt, y_sample = _combine(dest.reshape(-1), wgt, x1, mod, n_ffn_post.reshape(1, d), yb, nch, nb, seq)
    tail = slice(SUBLANES - (CONV_WIDTH - 1), SUBLANES)
    return y_prompt, y_sample, convp[:, tail], gdnp, glap, convs[:, tail], gdns, glas


def kernel(x_prompt, x_sample, c_prompt, c_sample, state_conv, state_gdn, state_gla, w_ada, b_ada, norm_mix_pre, norm_mix_post, norm_ffn_pre, norm_ffn_post, w_in, conv_w, gdn_a_log, gdn_dt_bias, gdn_norm, gla_gate_w2, gla_gate_b, gla_norm, w_branch, w_out, w_router, b_router, w_gate_up, b_gate_up, w_down, b_down):
    assert w_ada.shape[0] == 1, "single-layer step"
    weights = (w_ada, b_ada, norm_mix_pre, norm_mix_post, norm_ffn_pre, norm_ffn_post, w_in, conv_w,
               gdn_a_log, gdn_dt_bias, gdn_norm, gla_gate_w2, gla_gate_b, gla_norm, w_branch, w_out,
               w_router, b_router, w_gate_up, b_gate_up, w_down, b_down)
    yp, ys, cp, gp, lp, cs, gs, ls = _layer(
        x_prompt, x_sample, c_prompt, c_sample, state_conv[0], state_gdn[0], state_gla[0],
        *(w[0] for w in weights))
    return yp, ys, cp[None], gp[None], lp[None], cs[None], gs[None], ls[None]
```

```python
import functools

import jax
import jax.numpy as jnp
from jax import lax
from jax.experimental import pallas as pl
from jax.experimental.pallas import tpu as pltpu

F32 = jnp.float32
BF16 = jnp.bfloat16
I32 = jnp.int32
HIGHEST = lax.Precision.HIGHEST

CHUNK = 64
EPS = 1e-6
CONV_WIDTH = 4
TOP_K = 4
N_MOD = 6
GLA_GATE_TEMP = 16.0
GLA_LOG_DECAY_FLOOR = -1.0
SWIGLU_LIMIT = 7.0
SWIGLU_ALPHA = 1.702
LANES = 128
SUBLANES = 8
NEG_BIG = -1e30
VMEM_LIMIT = 60 * 1024 * 1024


def _cparams(sem, row_gather=False):
    return pltpu.CompilerParams(dimension_semantics=sem, vmem_limit_bytes=VMEM_LIMIT,
                                disable_bounds_checks=row_gather)


def _tile(n, target, quantum):
    if n <= target:
        return n
    t = target - target % quantum
    while t > quantum and n % t:
        t -= quantum
    assert n % t == 0, (n, target, quantum)
    return t


def _dot(a, b, precision=None):
    return jnp.dot(a, b, preferred_element_type=F32, precision=precision)


def _dot_nt(a, b, precision=None):
    return lax.dot_general(a, b, (((1,), (1,)), ((), ())), preferred_element_type=F32,
                           precision=precision)


def _silu(x):
    return x * jax.nn.sigmoid(x)


def _softplus(x):
    return jnp.maximum(x, 0.0) + jnp.log1p(jnp.exp(-jnp.abs(x)))


def _rms(x, g):
    return x * lax.rsqrt(jnp.mean(x * x, axis=-1, keepdims=True) + EPS) * g


def _ada_kernel(c_ref, w_ref, b_ref, o_ref):
    a = _silu(c_ref[...]).astype(BF16)
    o_ref[...] = _dot(a, w_ref[...].astype(BF16)) + b_ref[...]


def _ada_proj(c, w, b):
    m, k = c.shape
    n = w.shape[1]
    tn = _tile(n, 512, LANES)
    return pl.pallas_call(
        _ada_kernel, grid=(n // tn,),
        in_specs=[pl.BlockSpec((m, k), lambda j: (0, 0)),
                  pl.BlockSpec((k, tn), lambda j: (0, j)),
                  pl.BlockSpec((1, tn), lambda j: (0, j))],
        out_specs=pl.BlockSpec((m, tn), lambda j: (0, j)),
        out_shape=jax.ShapeDtypeStruct((m, n), F32),
        compiler_params=_cparams(("arbitrary",)), name="ada_proj")(c, w, b.reshape(1, n))


def _mm_kernel(a_ref, w_ref, o_ref):
    o_ref[...] = _dot(a_ref[...], w_ref[...]).astype(o_ref.dtype)


def _matmul(a, w, tm, tn, out_dtype, name):
    m, k = a.shape
    n = w.shape[1]
    return pl.pallas_call(
        _mm_kernel, grid=(n // tn, m // tm),
        in_specs=[pl.BlockSpec((tm, k), lambda j, i: (i, 0)),
                  pl.BlockSpec((k, tn), lambda j, i: (0, j))],
        out_specs=pl.BlockSpec((tm, tn), lambda j, i: (i, j)),
        out_shape=jax.ShapeDtypeStruct((m, n), out_dtype),
        compiler_params=_cparams(("arbitrary", "arbitrary")), name=name)(a, w)


def _xp_spec(d, nch):
    return pl.BlockSpec((None, CHUNK, d), lambda s, b: (b, jnp.minimum(s, nch - 1), 0))


def _xs_spec(d, nch):
    return pl.BlockSpec((None, CHUNK, d), lambda s, b: (jnp.where(s < nch, 0, b), 0, 0))


def _mod_spec(d, nch):
    return pl.BlockSpec((None, None, N_MOD, d), lambda s, b: (jnp.where(s < nch, 0, 1), b, 0, 0))


def _row_spec(width, nb, col=0):
    return pl.BlockSpec((CHUNK, width), lambda s, b: (s * nb + b, col))


def _prenorm_kernel(xp_ref, xs_ref, mod_ref, g_ref, h_ref, *, nch):
    s = pl.program_id(0)
    x = jnp.where(s < nch, xp_ref[...], xs_ref[...])
    shift, scale = mod_ref[0:1, :], mod_ref[1:2, :]
    h_ref[...] = (_rms(x, g_ref[...]) * (1.0 + scale) + shift).astype(h_ref.dtype)


def _prenorm(xp, xs, mod, g, nch):
    nb, _, d = xp.shape
    t = (nch + 1) * nb * CHUNK
    return pl.pallas_call(
        functools.partial(_prenorm_kernel, nch=nch), grid=(nch + 1, nb),
        in_specs=[_xp_spec(d, nch), _xs_spec(d, nch), _mod_spec(d, nch),
                  pl.BlockSpec((1, d), lambda s, b: (0, 0))],
        out_specs=_row_spec(d, nb),
        out_shape=jax.ShapeDtypeStruct((t, d), BF16),
        compiler_params=_cparams(("arbitrary", "arbitrary")), name="prenorm")(xp, xs, mod, g)


def _to_token_tiles(dst_ref, rows):
    n, w = rows.shape
    nt = w // LANES
    for c in range(nt):
        dst_ref[pl.ds(c, n, stride=nt), :] = rows[:, c * LANES:(c + 1) * LANES]


def _slot_rows(nt):
    return nt + 1


def _from_token_tiles(src_ref, n, nt, c):
    return src_ref[pl.ds(c, n, stride=_slot_rows(nt)), :]


def _midnorm_kernel(xp_ref, xs_ref, y_ref, mod_ref, g1_ref, g2_ref, x1_ref, h2_ref, h2t_ref, *, nch):
    s = pl.program_id(0)
    x = jnp.where(s < nch, xp_ref[...], xs_ref[...])
    gate1, shift2, scale2 = mod_ref[2:3, :], mod_ref[3:4, :], mod_ref[4:5, :]
    x1 = x + gate1 * _rms(y_ref[...], g1_ref[...])
    x1_ref[...] = x1
    h2 = _rms(x1, g2_ref[...]) * (1.0 + scale2) + shift2
    h2_ref[...] = h2
    _to_token_tiles(h2t_ref, h2)


def _midnorm(xp, xs, y, mod, g1, g2, nch):
    nb, _, d = xp.shape
    t = (nch + 1) * nb * CHUNK
    vec = pl.BlockSpec((1, d), lambda s, b: (0, 0))
    nt = d // LANES
    tiles = pl.BlockSpec((CHUNK * nt, LANES), lambda s, b: (s * nb + b, 0))
    return pl.pallas_call(
        functools.partial(_midnorm_kernel, nch=nch), grid=(nch + 1, nb),
        in_specs=[_xp_spec(d, nch), _xs_spec(d, nch), _row_spec(d, nb), _mod_spec(d, nch), vec, vec],
        out_specs=[_row_spec(d, nb), _row_spec(d, nb), tiles],
        out_shape=[jax.ShapeDtypeStruct((t, d), F32), jax.ShapeDtypeStruct((t, d), F32),
                   jax.ShapeDtypeStruct((t * nt, LANES), F32)],
        compiler_params=_cparams(("arbitrary", "arbitrary")), name="midnorm")(xp, xs, y, mod, g1, g2)


def _prompt_state_spec(block, nch, nb):
    tail = (0,) * (len(block) - 1)
    return pl.BlockSpec(block, lambda s, b: (jnp.where(s == nch - 1, b, jnp.where(s < nch - 1, 0, nb - 1)),) + tail)


def _sample_state_spec(block, nch):
    tail = (0,) * (len(block) - 1)
    return pl.BlockSpec(block, lambda s, b: (jnp.where(s == nch, b, 0),) + tail)


def _tri(n, strict=False, dtype=F32):
    i = lax.broadcasted_iota(I32, (n, n), 0)
    j = lax.broadcasted_iota(I32, (n, n), 1)
    return ((i > j) if strict else (i >= j)).astype(dtype)


GDN_HEAD_GROUP = 16
assert 2 * CHUNK == LANES


def _gdn_kernel(qkv_ref, z_ref, ab_ref, convw_ref, conv0_ref, alog_ref, dtb_ref, gn_ref, s0_ref,
                oa_ref, convp_ref, convs_ref, gdnp_ref, gdns_ref,
                tails, state, full, act, qn, kn, *, nch, hq, hv, dh, group):
    s = pl.program_id(0)
    b = pl.program_id(1)
    kd = hq * dh
    rep = hv // hq

    @pl.when(s == 0)
    def _():
        tails[b] = jnp.zeros(tails.shape[1:], F32)
        state[b] = jnp.zeros(state.shape[1:], F32)

    @pl.when(s == nch)
    def _():
        tails[b] = conv0_ref[...]
        state[b] = s0_ref[...]

    full[0:SUBLANES, :] = tails[b]
    full[SUBLANES:SUBLANES + CHUNK, :] = qkv_ref[...]
    base = SUBLANES - (CONV_WIDTH - 1)
    acc = full[base:base + CHUNK, :] * convw_ref[0:1, :]
    for i in range(1, CONV_WIDTH):
        acc = acc + full[base + i:base + i + CHUNK, :] * convw_ref[i:i + 1, :]
    act[...] = _silu(acc)
    new_tail = full[CHUNK:CHUNK + SUBLANES, :]
    tails[b] = new_tail

    for h in range(hq):
        q = act[:, h * dh:(h + 1) * dh]
        k = act[:, kd + h * dh:kd + (h + 1) * dh]
        qn[:, h * dh:(h + 1) * dh] = q * lax.rsqrt(jnp.sum(q * q, axis=-1, keepdims=True) + EPS) * dh ** -0.5
        kn[:, h * dh:(h + 1) * dh] = k * lax.rsqrt(jnp.sum(k * k, axis=-1, keepdims=True) + EPS)

    ab = ab_ref[...]
    g_all = -jnp.exp(alog_ref[...]) * _softplus(ab + dtb_ref[...])
    beta_all = jax.nn.sigmoid(ab)
    gc_all = _dot(_tri(CHUNK), g_all, HIGHEST)
    gc_t = jnp.concatenate([gc_all, gc_all], axis=0).T
    eg_all = jnp.exp(gc_all)
    g_last = gc_all[CHUNK - 1:CHUNK, :]
    ekd_all = jnp.exp(g_last - gc_all)
    edec_all = jnp.exp(g_last)

    ii = lax.broadcasted_iota(I32, (CHUNK, LANES), 0)
    lane = lax.broadcasted_iota(I32, (CHUNK, LANES), 1)
    first = lane < CHUNK
    jj = jnp.where(first, lane, lane - CHUNK)
    incl = ii >= jj
    strict = ii > jj
    eye = (ii == jj).astype(F32)
    zeros_bf = jnp.zeros((CHUNK, LANES), BF16)

    def halves(x, with_rhs=True):
        hi = x.astype(BF16)
        rest = x - hi.astype(F32)
        rhs = None
        if with_rhs:
            rhs = jnp.concatenate([jnp.where(first, x, rest).astype(BF16), zeros_bf], axis=0)
        return hi, rest.astype(BF16), rhs

    def fold(r):
        sm = r[:CHUNK] + r[CHUNK:]
        return sm + pltpu.roll(sm, CHUNK, axis=1)

    for h0 in range(0, hv, group):
        heads = range(h0, min(h0 + group, hv))
        kh, qh, kb, kcat, dec, nn, tm = {}, {}, {}, {}, {}, {}, {}
        for h in heads:
            kh[h] = kn[:, (h // rep) * dh:(h // rep + 1) * dh]
            qh[h] = qn[:, (h // rep) * dh:(h // rep + 1) * dh]
            kcat[h] = jnp.concatenate([kh[h], kh[h]], axis=0).astype(BF16)
            beta = beta_all[:, hv + h:hv + h + 1]
            kb[h] = kh[h] * beta
            diff = gc_all[:, h:h + 1] - gc_t[h:h + 1, :]
            dec[h] = jnp.exp(jnp.where(incl, diff, NEG_BIG))
        for h in heads:
            a = _dot_nt(kb[h].astype(BF16), kcat[h])
            nn[h] = jnp.where(strict, -(a * dec[h]), 0.0)
            tm[h] = eye + nn[h]
        for h in heads:
            p_hi, p_lo, rhs = halves(nn[h])
            nn[h] = fold(_dot(jnp.concatenate([p_hi, p_lo], axis=0), rhs))
        p = 4
        while p < CHUNK:
            for h in heads:
                p_hi, p_lo, rhs = halves(nn[h])
                t_hi, t_lo, _ = halves(tm[h], with_rhs=False)
                r = _dot(jnp.concatenate([p_hi, p_lo, t_hi, t_lo], axis=0), rhs)
                nn[h] = fold(r[:2 * CHUNK])
                tm[h] = tm[h] + fold(r[2 * CHUNK:])
            p *= 2
        for h in heads:
            _, _, rhs = halves(nn[h])
            t_hi, t_lo, _ = halves(tm[h], with_rhs=False)
            tm[h] = tm[h] + fold(_dot(jnp.concatenate([t_hi, t_lo], axis=0), rhs))
        u, w, attn = {}, {}, {}
        for h in heads:
            beta = beta_all[:, hv + h:hv + h + 1]
            vh = act[:, 2 * kd + h * dh:2 * kd + (h + 1) * dh]
            rhs = jnp.concatenate([vh * beta, kb[h] * eg_all[:, h:h + 1]], axis=-1).astype(BF16)
            sol = _dot(tm[h][:, :CHUNK].astype(BF16), rhs)
            u[h], w[h] = sol[:, :dh], sol[:, dh:]
            attn[h] = (_dot_nt(qh[h].astype(BF16), kcat[h]) * dec[h])[:, :CHUNK].astype(BF16)
        vnew, o = {}, {}
        for h in heads:
            lhs = jnp.concatenate([w[h], qh[h] * eg_all[:, h:h + 1]], axis=0).astype(BF16)
            ws_qs = _dot(lhs, state[b, h].astype(BF16))
            vnew[h] = u[h] - ws_qs[:CHUNK]
            o[h] = ws_qs[CHUNK:] + _dot(attn[h], vnew[h].astype(BF16))
        for h in heads:
            kdec_t = (kh[h] * ekd_all[:, h:h + 1]).T.astype(BF16)
            state[b, h] = state[b, h] * edec_all[0:1, h:h + 1] + _dot(kdec_t, vnew[h].astype(BF16))
        for h in heads:
            zh = z_ref[:, h * dh:(h + 1) * dh]
            oa_ref[:, h * dh:(h + 1) * dh] = (_rms(o[h], gn_ref[...]) * _silu(zh)).astype(oa_ref.dtype)

    @pl.when(s == nch - 1)
    def _():
        convp_ref[...] = new_tail
        gdnp_ref[...] = state[b]

    @pl.when(s == nch)
    def _():
        convs_ref[...] = new_tail
        gdns_ref[...] = state[b]


def _gdn(p_main, p_small, conv_w, conv0, a_log, dt_bias, gdn_norm, s0, *, nch, nb, hq, hv, dh, z_col):
    t = p_main.shape[0]
    cd = conv_w.shape[1]
    vd = hv * dh
    kd = hq * dh
    vec = lambda n: pl.BlockSpec((1, n), lambda s, b: (0, 0))
    st_block = (None, hv, dh, dh)
    cv_block = (None, SUBLANES, cd)
    return pl.pallas_call(
        functools.partial(_gdn_kernel, nch=nch, hq=hq, hv=hv, dh=dh, group=GDN_HEAD_GROUP), grid=(nch + 1, nb),
        in_specs=[_row_spec(cd, nb), _row_spec(vd, nb, z_col), _row_spec(LANES, nb),
                  pl.BlockSpec((CONV_WIDTH, cd), lambda s, b: (0, 0)),
                  _sample_state_spec(cv_block, nch), vec(LANES), vec(LANES), vec(dh),
                  _sample_state_spec(st_block, nch)],
        out_specs=[_row_spec(vd, nb),
                   _prompt_state_spec(cv_block, nch, nb), _sample_state_spec(cv_block, nch),
                   _prompt_state_spec(st_block, nch, nb), _sample_state_spec(st_block, nch)],
        out_shape=[jax.ShapeDtypeStruct((t, vd), BF16),
                   jax.ShapeDtypeStruct((nb, SUBLANES, cd), F32), jax.ShapeDtypeStruct((nb, SUBLANES, cd), F32),
                   jax.ShapeDtypeStruct((nb, hv, dh, dh), F32), jax.ShapeDtypeStruct((nb, hv, dh, dh), F32)],
        scratch_shapes=[pltpu.VMEM((nb, SUBLANES, cd), F32), pltpu.VMEM((nb, hv, dh, dh), F32),
                        pltpu.VMEM((SUBLANES + CHUNK, cd), F32), pltpu.VMEM((CHUNK, cd), F32),
                        pltpu.VMEM((CHUNK, kd), F32), pltpu.VMEM((CHUNK, kd), F32)],
        compiler_params=_cparams(("arbitrary", "arbitrary")), name="gdn")(
            p_main, p_main, p_small, conv_w, conv0, a_log, dt_bias, gdn_norm, s0)


def _gla_kernel(q_ref, k_ref, v_ref, r_ref, ab_ref, w2_ref, gb_ref, gn_ref, s0_ref,
                ob_ref, glap_ref, glas_ref, state, *, nch, nh, hk, hvd):
    s = pl.program_id(0)
    b = pl.program_id(1)

    @pl.when(s == 0)
    def _():
        state[b] = jnp.zeros(state.shape[1:], F32)

    @pl.when(s == nch)
    def _():
        state[b] = s0_ref[...]

    pre = _dot(ab_ref[...], w2_ref[...], HIGHEST) + gb_ref[...]
    log_sig = jnp.minimum(pre, 0.0) - jnp.log1p(jnp.exp(-jnp.abs(pre)))
    gk = jnp.maximum(log_sig / GLA_GATE_TEMP, GLA_LOG_DECAY_FLOOR)
    bc = _dot(_tri(CHUNK), gk, HIGHEST)
    b_last = bc[CHUNK - 1:CHUNK, :]
    qd = (q_ref[...] * hk ** -0.5 * jnp.exp(bc)).astype(BF16)
    ki = (k_ref[...] * jnp.exp(-bc)).astype(BF16)
    kdec = k_ref[...] * jnp.exp(b_last - bc)
    edec = jnp.exp(b_last)
    incl = _tri(CHUNK) > 0.0

    for h in range(nh):
        ks = slice(h * hk, (h + 1) * hk)
        vs = slice(h * hvd, (h + 1) * hvd)
        vh = v_ref[:, vs].astype(BF16)
        attn = jnp.where(incl, _dot_nt(qd[:, ks], ki[:, ks]), 0.0).astype(BF16)
        o = _dot(attn, vh) + _dot(qd[:, ks], state[b, h].astype(BF16))
        edec_col = jnp.broadcast_to(edec[:, ks], (SUBLANES, hk)).T[:, 0:1]
        state[b, h] = state[b, h] * edec_col + _dot(kdec[:, ks].T.astype(BF16), vh)
        ob_ref[:, vs] = (_rms(o, gn_ref[...]) * _silu(r_ref[:, vs])).astype(ob_ref.dtype)

    @pl.when(s == nch - 1)
    def _():
        glap_ref[...] = state[b]

    @pl.when(s == nch)
    def _():
        glas_ref[...] = state[b]


def _gla(p_main, p_small, w2_pad, gate_b, gla_norm, s0, *, nch, nb, nh, hk, hvd, cols):
    t = p_main.shape[0]
    kl, vl = nh * hk, nh * hvd
    st_block = (None, nh, hk, hvd)
    q_col, k_col, v_col, r_col = cols
    return pl.pallas_call(
        functools.partial(_gla_kernel, nch=nch, nh=nh, hk=hk, hvd=hvd), grid=(nch + 1, nb),
        in_specs=[_row_spec(kl, nb, q_col), _row_spec(kl, nb, k_col), _row_spec(vl, nb, v_col),
                  _row_spec(vl, nb, r_col), _row_spec(LANES, nb),
                  pl.BlockSpec((LANES, kl), lambda s, b: (0, 0)),
                  pl.BlockSpec((1, kl), lambda s, b: (0, 0)),
                  pl.BlockSpec((1, hvd), lambda s, b: (0, 0)),
                  _sample_state_spec(st_block, nch)],
        out_specs=[_row_spec(vl, nb), _prompt_state_spec(st_block, nch, nb), _sample_state_spec(st_block, nch)],
        out_shape=[jax.ShapeDtypeStruct((t, vl), BF16),
                   jax.ShapeDtypeStruct((nb, nh, hk, hvd), F32), jax.ShapeDtypeStruct((nb, nh, hk, hvd), F32)],
        scratch_shapes=[pltpu.VMEM((nb, nh, hk, hvd), F32)],
        compiler_params=_cparams(("arbitrary", "arbitrary")), name="gla")(
            p_main, p_main, p_main, p_main, p_small, w2_pad, gate_b, gla_norm, s0)


def _merge_kernel(oa_ref, ob_ref, wa_ref, wb_ref, g0_ref, g1_ref, o_ref):
    ya = _dot(oa_ref[...], wa_ref[...])
    yb = _dot(ob_ref[...], wb_ref[...])
    o_ref[...] = (jax.nn.sigmoid(g0_ref[...]) * ya + jax.nn.sigmoid(g1_ref[...]) * yb).astype(o_ref.dtype)


def _merge(oa, ob, wa, wb, p_main, br_off, tm, tn):
    t, ka = oa.shape
    kb = ob.shape[1]
    d = wa.shape[1]
    c0, c1 = br_off // tn, (br_off + d) // tn
    return pl.pallas_call(
        _merge_kernel, grid=(d // tn, t // tm),
        in_specs=[pl.BlockSpec((tm, ka), lambda j, i: (i, 0)), pl.BlockSpec((tm, kb), lambda j, i: (i, 0)),
                  pl.BlockSpec((ka, tn), lambda j, i: (0, j)), pl.BlockSpec((kb, tn), lambda j, i: (0, j)),
                  pl.BlockSpec((tm, tn), lambda j, i: (i, c0 + j)), pl.BlockSpec((tm, tn), lambda j, i: (i, c1 + j))],
        out_specs=pl.BlockSpec((tm, tn), lambda j, i: (i, j)),
        out_shape=jax.ShapeDtypeStruct((t, d), BF16),
        compiler_params=_cparams(("arbitrary", "arbitrary")), name="merge")(oa, ob, wa, wb, p_main, p_main)


def _router_kernel(h_ref, w_ref, b_ref, idx_ref, wgt_ref, rank_ref, cnt_ref, run, *, tm):
    @pl.when(pl.program_id(0) == 0)
    def _():
        run[...] = jnp.zeros(run.shape, F32)

    logits = _dot(h_ref[...], w_ref[...], HIGHEST) + b_ref[...]
    lane = lax.broadcasted_iota(I32, logits.shape, 1).astype(F32)
    cur = logits
    tops, sels, idxs = [], [], []
    for _ in range(TOP_K):
        m = jnp.max(cur, axis=-1, keepdims=True)
        idx = jnp.min(jnp.where(cur == m, lane, float(LANES)), axis=-1, keepdims=True)
        sel = lane == idx
        tops.append(m)
        sels.append(sel)
        idxs.append(idx)
        cur = jnp.where(sel, 2.0 * NEG_BIG, cur)
    exps = [jnp.exp(m - tops[0]) for m in tops]
    denom = exps[0]
    for e in exps[1:]:
        denom = denom + e
    onehot = sels[0].astype(F32)
    for sel in sels[1:]:
        onehot = onehot + sel.astype(F32)
    prefix = _dot(_tri(tm, strict=True, dtype=BF16), onehot.astype(BF16)) + run[0:1, :]
    idx_out = jnp.zeros(logits.shape, F32)
    wgt_out = jnp.zeros(logits.shape, F32)
    rank_out = jnp.zeros(logits.shape, F32)
    for k in range(TOP_K):
        rank = jnp.sum(jnp.where(sels[k], prefix, 0.0), axis=-1, keepdims=True)
        idx_out = jnp.where(lane == float(k), idxs[k], idx_out)
        wgt_out = jnp.where(lane == float(k), exps[k] / denom, wgt_out)
        rank_out = jnp.where(lane == float(k), rank, rank_out)
    idx_ref[...] = idx_out.astype(I32)
    wgt_ref[...] = wgt_out
    rank_ref[...] = rank_out.astype(I32)
    total = run[...] + jnp.sum(onehot, axis=0, keepdims=True)
    run[...] = total
    cnt_ref[...] = total.astype(I32)


def _router(h2, w_pad, b_pad, tm):
    t, d = h2.shape
    row = pl.BlockSpec((tm, LANES), lambda i: (i, 0))
    return pl.pallas_call(
        functools.partial(_router_kernel, tm=tm), grid=(t // tm,),
        in_specs=[pl.BlockSpec((tm, d), lambda i: (i, 0)), pl.BlockSpec((d, LANES), lambda i: (0, 0)),
                  pl.BlockSpec((1, LANES), lambda i: (0, 0))],
        out_specs=[row, row, row, pl.BlockSpec((SUBLANES, LANES), lambda i: (0, 0))],
        out_shape=[jax.ShapeDtypeStruct((t, LANES), I32), jax.ShapeDtypeStruct((t, LANES), F32),
                   jax.ShapeDtypeStruct((t, LANES), I32), jax.ShapeDtypeStruct((SUBLANES, LANES), I32)],
        scratch_shapes=[pltpu.VMEM((SUBLANES, LANES), F32)],
        compiler_params=_cparams(("arbitrary",)), name="router")(h2, w_pad, b_pad)


def _row_copy(src_hbm, dst, row, slot, nt, sem):
    return pltpu.make_async_copy(src_hbm.at[row], dst.at[pl.ds(slot * _slot_rows(nt), nt)], sem)


GATHER_UNROLL = 8


def _dispatch_kernel(nu_ref, tok_ref, src_hbm, o_ref, buf, sem, *, tg, nt):
    i = pl.program_id(0)

    @pl.when(i < nu_ref[0])
    def _():
        def issue(r, c):
            _row_copy(src_hbm, buf, tok_ref[r], r, nt, sem).start()
            return c

        def drain(r, c):
            _row_copy(src_hbm, buf, tok_ref[r], r, nt, sem).wait()
            return c

        lax.fori_loop(0, tg, issue, 0, unroll=GATHER_UNROLL)
        lax.fori_loop(0, tg, drain, 0, unroll=GATHER_UNROLL)
        for c in range(nt):
            o_ref[:, c * LANES:(c + 1) * LANES] = _from_token_tiles(buf, tg, nt, c).astype(o_ref.dtype)

    @pl.when(i >= nu_ref[0])
    def _():
        o_ref[...] = jnp.zeros(o_ref.shape, o_ref.dtype)


def _dispatch(row_token, n_used, h2_tiles, tg):
    r = row_token.shape[0]
    nt = h2_tiles.shape[1]
    d = nt * LANES
    return pl.pallas_call(
        functools.partial(_dispatch_kernel, tg=tg, nt=nt),
        grid_spec=pltpu.PrefetchScalarGridSpec(
            num_scalar_prefetch=1, grid=(r // tg,),
            in_specs=[pl.BlockSpec((tg,), lambda i, nu: (i,), memory_space=pltpu.SMEM),
                      pl.BlockSpec(memory_space=pl.ANY)],
            out_specs=pl.BlockSpec((tg, d), lambda i, nu: (i, 0)),
            scratch_shapes=[pltpu.VMEM((tg * _slot_rows(nt), LANES), F32), pltpu.SemaphoreType.DMA(())]),
        out_shape=jax.ShapeDtypeStruct((r, d), BF16),
        compiler_params=_cparams(("arbitrary",), row_gather=True), name="moe_dispatch")(
            n_used, row_token, h2_tiles)


META_FIELDS = 4


def _weight_copy(w_hbm, land, sems, slot, part, expert, col, tn):
    src = w_hbm.at[expert, :, pl.ds(pl.multiple_of(col, tn), tn)]
    return pltpu.make_async_copy(src, land.at[slot, part], sems.at[slot, part])


def _swap_in_weights(be_ref, nu_ref, meta_ref, w_hbm, land, sems, dst_bf, col_of):
    j, i = pl.program_id(0), pl.program_id(1)
    n_sweeps = pl.num_programs(0)
    tn = dst_bf[0].shape[1]
    parts = range(len(dst_bf))

    @pl.when(jnp.logical_and(i < nu_ref[0], meta_ref[META_FIELDS * i] == 1))
    def _():
        run = j * nu_ref[1] + meta_ref[META_FIELDS * i + 1]
        slot = run % 2

        @pl.when(run == 0)
        def _():
            for p in parts:
                _weight_copy(w_hbm, land, sems, slot, p, be_ref[i], col_of(p, j), tn).start()

        for p in parts:
            _weight_copy(w_hbm, land, sems, slot, p, be_ref[i], col_of(p, j), tn).wait()
            dst_bf[p][...] = land[slot, p].astype(BF16)

        next_sweep = j + meta_ref[META_FIELDS * i + 2]

        @pl.when(next_sweep < n_sweeps)
        def _():
            for p in parts:
                _weight_copy(w_hbm, land, sems, 1 - slot, p, meta_ref[META_FIELDS * i + 3],
                             col_of(p, next_sweep), tn).start()


def _for_filled_rows(i, nu_ref, half_ref, o_ref, compute):
    tm = o_ref.shape[0]
    used = i < nu_ref[0]
    half_full = half_ref[i] == 1

    @pl.when(jnp.logical_and(used, jnp.logical_not(half_full)))
    def _():
        compute(slice(0, tm))

    @pl.when(jnp.logical_and(used, half_full))
    def _():
        compute(slice(0, tm // 2))
        o_ref[tm // 2:] = jnp.zeros((tm - tm // 2,) + o_ref.shape[1:], o_ref.dtype)

    @pl.when(jnp.logical_not(used))
    def _():
        o_ref[...] = jnp.zeros(o_ref.shape, o_ref.dtype)


def _gate_up_kernel(be_ref, nu_ref, half_ref, meta_ref, x_ref, w_hbm, bg_ref, bl_ref, o_ref,
                    land, wg_bf, wl_bf, sems):
    i = pl.program_id(1)
    tn = wg_bf.shape[1]
    n_sweeps = pl.num_programs(0)
    _swap_in_weights(be_ref, nu_ref, meta_ref, w_hbm, land, sems, (wg_bf, wl_bf),
                     lambda part, sweep: (part * n_sweeps + sweep) * tn)

    def compute(rows):
        x = x_ref[rows, :]
        glu = jnp.minimum(_dot(x, wg_bf[...]) + bg_ref[...], SWIGLU_LIMIT)
        lin = jnp.clip(_dot(x, wl_bf[...]) + bl_ref[...], -SWIGLU_LIMIT, SWIGLU_LIMIT)
        o_ref[rows, :] = (glu * jax.nn.sigmoid(SWIGLU_ALPHA * glu) * (lin + 1.0)).astype(o_ref.dtype)

    _for_filled_rows(i, nu_ref, half_ref, o_ref, compute)


def _down_kernel(be_ref, nu_ref, half_ref, meta_ref, a_ref, w_hbm, b_ref, o_ref, land, w_bf, sems):
    i = pl.program_id(1)
    tn = w_bf.shape[1]
    _swap_in_weights(be_ref, nu_ref, meta_ref, w_hbm, land, sems, (w_bf,), lambda part, sweep: sweep * tn)

    def compute(rows):
        o_ref[rows, :] = _dot(a_ref[rows, :], w_bf[...]) + b_ref[...]

    _for_filled_rows(i, nu_ref, half_ref, o_ref, compute)


def _moe_experts(xs, block_expert, n_used, block_half, meta, w_gate_up, b_gate_up, w_down, b_down, tm):
    r, d = xs.shape
    ne, _, two_de = w_gate_up.shape
    de = two_de // 2
    nb = r // tm
    tn = _tile(de, 512, LANES)
    nj = de // tn

    def blk(i, nu):
        return jnp.maximum(jnp.minimum(i, nu[0] - 1), 0)

    hbm = pl.BlockSpec(memory_space=pl.ANY)
    act = pl.pallas_call(
        _gate_up_kernel,
        grid_spec=pltpu.PrefetchScalarGridSpec(
            num_scalar_prefetch=4, grid=(nj, nb),
            in_specs=[pl.BlockSpec((tm, d), lambda j, i, be, nu, hf, mt: (blk(i, nu), 0)), hbm,
                      pl.BlockSpec((None, 1, tn), lambda j, i, be, nu, hf, mt: (be[blk(i, nu)], 0, j)),
                      pl.BlockSpec((None, 1, tn), lambda j, i, be, nu, hf, mt: (be[blk(i, nu)], 0, nj + j))],
            out_specs=pl.BlockSpec((tm, tn), lambda j, i, be, nu, hf, mt: (i, j)),
            scratch_shapes=[pltpu.VMEM((2, 2, d, tn), F32), pltpu.VMEM((d, tn), BF16), pltpu.VMEM((d, tn), BF16),
                            pltpu.SemaphoreType.DMA((2, 2))]),
        out_shape=jax.ShapeDtypeStruct((r, de), BF16),
        compiler_params=_cparams(("arbitrary", "arbitrary")), name="moe_gate_up")(
            block_expert, n_used, block_half, meta, xs, w_gate_up,
            b_gate_up.reshape(ne, 1, two_de), b_gate_up.reshape(ne, 1, two_de))

    dm = w_down.shape[2]
    tn = _tile(dm, 1024, LANES)
    njd = dm // tn
    return pl.pallas_call(
        _down_kernel,
        grid_spec=pltpu.PrefetchScalarGridSpec(
            num_scalar_prefetch=4, grid=(njd, nb),
            in_specs=[pl.BlockSpec((tm, de), lambda j, i, be, nu, hf, mt: (blk(i, nu), 0)), hbm,
                      pl.BlockSpec((None, 1, tn), lambda j, i, be, nu, hf, mt: (be[blk(i, nu)], 0, j))],
            out_specs=pl.BlockSpec((tm, tn), lambda j, i, be, nu, hf, mt: (i, j)),
            scratch_shapes=[pltpu.VMEM((2, 1, de, tn), F32), pltpu.VMEM((de, tn), BF16),
                            pltpu.SemaphoreType.DMA((2, 1))]),
        out_shape=jax.ShapeDtypeStruct((r, dm), F32),
        compiler_params=_cparams(("arbitrary", "arbitrary")), name="moe_down")(
            block_expert, n_used, block_half, meta, act, w_down, b_down.reshape(ne, 1, dm))


def _strided_row_copy(src_hbm, dst, row, slot, sem):
    return pltpu.make_async_copy(src_hbm.at[pl.ds(row, 1)], dst.at[pl.ds(slot, 1)], sem)


def _combine_kernel(dest_ref, wgt_ref, x1_ref, mod_ref, g_ref, yb_hbm, op_ref, os_ref, buf, sem, *, nch):
    s = pl.program_id(0)

    def issue(n, c):
        t, k = n // TOP_K, n % TOP_K
        _strided_row_copy(yb_hbm, buf.at[k], dest_ref[n], t, sem).start()
        return c

    def drain(n, c):
        t, k = n // TOP_K, n % TOP_K
        _strided_row_copy(yb_hbm, buf.at[k], dest_ref[n], t, sem).wait()
        return c

    lax.fori_loop(0, CHUNK * TOP_K, issue, 0, unroll=GATHER_UNROLL)
    lax.fori_loop(0, CHUNK * TOP_K, drain, 0, unroll=GATHER_UNROLL)
    wgt = wgt_ref[...]
    y = wgt[:, 0:1] * buf[0]
    for k in range(1, TOP_K):
        y = y + wgt[:, k:k + 1] * buf[k]
    out = x1_ref[...] + mod_ref[5:6, :] * _rms(y, g_ref[...])

    @pl.when(s < nch)
    def _():
        op_ref[...] = out

    @pl.when(s == nch)
    def _():
        os_ref[...] = out


def _combine(dest, wgt, x1, mod, g, yb, nch, nb, seq):
    t, d = x1.shape
    return pl.pallas_call(
        functools.partial(_combine_kernel, nch=nch), grid=(nch + 1, nb),
        in_specs=[pl.BlockSpec((CHUNK * TOP_K,), lambda s, b: (s * nb + b,), memory_space=pltpu.SMEM),
                  _row_spec(LANES, nb), _row_spec(d, nb), _mod_spec(d, nch),
                  pl.BlockSpec((1, d), lambda s, b: (0, 0)),
                  pl.BlockSpec(memory_space=pl.ANY)],
        out_specs=[pl.BlockSpec((None, CHUNK, d),
                                lambda s, b: (jnp.where(s < nch, b, nb - 1), jnp.minimum(s, nch - 1), 0)),
                   pl.BlockSpec((None, CHUNK, d), lambda s, b: (jnp.where(s == nch, b, 0), 0, 0))],
        out_shape=[jax.ShapeDtypeStruct((nb, seq, d), F32), jax.ShapeDtypeStruct((nb, CHUNK, d), F32)],
        scratch_shapes=[pltpu.VMEM((TOP_K, CHUNK, d), F32), pltpu.SemaphoreType.DMA(())],
        compiler_params=_cparams(("arbitrary", "arbitrary"), row_gather=True), name="moe_combine")(
            dest, wgt, x1, mod, g, yb)


def _layer(x_prompt, x_sample, c_prompt, c_sample, conv_s0, gdn_s0, gla_s0,
           w_ada, b_ada, n_mix_pre, n_mix_post, n_ffn_pre, n_ffn_post, w_in, conv_w, a_log, dt_bias,
           gdn_norm, gla_w2, gla_b, gla_norm, w_branch, w_out, w_router, b_router,
           w_gate_up, b_gate_up, w_down, b_down):
    nb, seq, d = x_prompt.shape
    assert x_sample.shape == (nb, CHUNK, d) and seq % CHUNK == 0
    nch = seq // CHUNK
    t = (nch + 1) * nb * CHUNK
    hv, dh = a_log.shape[0], gdn_norm.shape[0]
    cd = conv_w.shape[1]
    vd = hv * dh
    kd = (cd - vd) // 2
    hq = kd // dh
    nh, hk, hvd = gla_s0.shape[1:]
    kl, vl = nh * hk, nh * hvd
    rank = gla_w2.shape[0]
    ne = w_router.shape[1]
    assert 2 * hv + rank <= LANES and ne <= LANES

    c_all = jnp.concatenate([c_prompt, c_sample], axis=0)
    mod = _ada_proj(c_all, w_ada, b_ada).reshape(2, nb, N_MOD, d)

    sizes = (cd, hv, hv, vd, kl, kl, vl, rank, vl, 2 * d)
    offs = [0]
    for sz in sizes:
        offs.append(offs[-1] + sz)
    seg = lambda i: w_in[:, offs[i]:offs[i + 1]]
    main_ids = (0, 3, 4, 5, 6, 8, 9)
    w_main = jnp.concatenate([seg(i) for i in main_ids], axis=1).astype(BF16)
    n_small = 2 * hv + rank
    w_small = jnp.concatenate([seg(1), seg(2), seg(7), jnp.zeros((d, LANES - n_small), F32)], axis=1).astype(BF16)
    moff = [0]
    for i in main_ids:
        moff.append(moff[-1] + sizes[i])
    z_off, lq_off, lk_off, lv_off, lr_off, br_off = moff[1:7]
    assert z_off % vd == 0 and lq_off % kl == 0 and lk_off % kl == 0 and lv_off % vl == 0 and lr_off % vl == 0

    h1 = _prenorm(x_prompt, x_sample, mod, n_mix_pre.reshape(1, d), nch)
    tm = _tile(t, 512, CHUNK)
    p_main = _matmul(h1, w_main, tm, _tile(w_main.shape[1], 1024, LANES), F32, "in_proj")
    p_small = _matmul(h1, w_small, tm, LANES, F32, "in_proj_small")

    pad_lanes = lambda v: jnp.pad(v.reshape(1, -1), ((0, 0), (0, LANES - v.shape[0])))
    conv0 = jnp.pad(conv_s0, ((0, 0), (SUBLANES - (CONV_WIDTH - 1), 0), (0, 0)))
    oa, convp, convs, gdnp, gdns = _gdn(
        p_main, p_small, conv_w, conv0, pad_lanes(a_log), pad_lanes(dt_bias), gdn_norm.reshape(1, dh), gdn_s0,
        nch=nch, nb=nb, hq=hq, hv=hv, dh=dh, z_col=z_off // vd)
    w2_pad = jnp.zeros((LANES, kl), F32).at[2 * hv:2 * hv + rank].set(gla_w2)
    ob, glap, glas = _gla(
        p_main, p_small, w2_pad, gla_b.reshape(1, kl), gla_norm.reshape(1, hvd), gla_s0,
        nch=nch, nb=nb, nh=nh, hk=hk, hvd=hvd,
        cols=(lq_off // kl, lk_off // kl, lv_off // vl, lr_off // vl))

    tn = _tile(d, 1024, LANES)
    assert br_off % tn == 0
    ym = _merge(oa, ob, w_branch[:vd].astype(BF16), w_branch[vd:].astype(BF16), p_main, br_off, tm, tn)
    y2 = _matmul(ym, w_out.astype(BF16), tm, tn, F32, "out_proj")
    x1, h2, h2_tiles = _midnorm(x_prompt, x_sample, y2, mod, n_mix_post.reshape(1, d), n_ffn_pre.reshape(1, d), nch)

    w_r = jnp.pad(w_router, ((0, 0), (0, LANES - ne)))
    b_r = jnp.pad(b_router.reshape(1, ne), ((0, 0), (0, LANES - ne)), constant_values=NEG_BIG)
    idx, wgt, rnk, cnt = _router(h2, w_r, b_r, tm)
    tmm = _tile(t * TOP_K, 512, CHUNK)
    counts = cnt[0, :ne]
    padded = (counts + tmm - 1) // tmm * tmm
    pad_end = jnp.cumsum(padded)
    pad_start = pad_end - padded
    dest = (pad_start[idx[:, :TOP_K]] + rnk[:, :TOP_K]).astype(I32)
    n_blocks = t * TOP_K // tmm + ne
    n_rows = n_blocks * tmm
    token_of_pair = jnp.repeat(jnp.arange(t, dtype=I32), TOP_K)
    row_token = jnp.zeros((n_rows,), I32).at[dest.reshape(-1)].set(token_of_pair)
    block_start = jnp.arange(n_blocks, dtype=I32) * tmm
    block_expert = jnp.minimum(jnp.sum(block_start[:, None] >= pad_end[None, :], axis=1), ne - 1).astype(I32)
    filled = counts[block_expert] - (block_start - pad_start[block_expert])
    block_half = (filled <= tmm // 2).astype(I32)
    present = counts > 0
    ids = jnp.arange(ne, dtype=I32)
    later = jnp.logical_and(ids[None, :] > ids[:, None], present[None, :])
    has_later = jnp.any(later, axis=1)
    next_expert = jnp.where(has_later, jnp.argmax(later, axis=1), jnp.argmax(present)).astype(I32)
    run_of_expert = (jnp.cumsum(present) - 1).astype(I32)
    n_used_blocks = pad_end[-1] // tmm
    n_used = jnp.stack([n_used_blocks, jnp.sum(present)]).astype(I32)
    prev_expert = jnp.concatenate([jnp.full((1,), -1, I32), block_expert[:-1]])
    first = jnp.logical_and(block_expert != prev_expert, block_start < pad_end[-1])
    meta = jnp.stack([first.astype(I32), run_of_expert[block_expert],
                      jnp.logical_not(has_later)[block_expert].astype(I32), next_expert[block_expert]],
                     axis=1).reshape(-1)
    assert meta.shape[0] == META_FIELDS * n_blocks

    xs = _dispatch(row_token, n_used, h2_tiles.reshape(t, d // LANES, LANES), tmm)
    yb = _moe_experts(xs, block_expert, n_used, block_half, meta, w_gate_up, b_gate_up, w_down, b_down, tmm)
    y_prompt, y_sample = _combine(dest.reshape(-1), wgt, x1, mod, n_ffn_post.reshape(1, d), yb, nch, nb, seq)
    tail = slice(SUBLANES - (CONV_WIDTH - 1), SUBLANES)
    return y_prompt, y_sample, convp[:, tail], gdnp, glap, convs[:, tail], gdns, glas


def kernel(x_prompt, x_sample, c_prompt, c_sample, state_conv, state_gdn, state_gla, w_ada, b_ada, norm_mix_pre, norm_mix_post, norm_ffn_pre, norm_ffn_post, w_in, conv_w, gdn_a_log, gdn_dt_bias, gdn_norm, gla_gate_w2, gla_gate_b, gla_norm, w_branch, w_out, w_router, b_router, w_gate_up, b_gate_up, w_down, b_down):
    assert w_ada.shape[0] == 1, "single-layer step"
    weights = (w_ada, b_ada, norm_mix_pre, norm_mix_post, norm_ffn_pre, norm_ffn_post, w_in, conv_w,
               gdn_a_log, gdn_dt_bias, gdn_norm, gla_gate_w2, gla_gate_b, gla_norm, w_branch, w_out,
               w_router, b_router, w_gate_up, b_gate_up, w_down, b_down)
    yp, ys, cp, gp, lp, cs, gs, ls = _layer(
        x_prompt, x_sample, c_prompt, c_sample, state_conv[0], state_gdn[0], state_gla[0],
        *(w[0] for w in weights))
    return yp, ys, cp[None], gp[None], lp[None], cs[None], gs[None], ls[None]
```

```python
import functools

import jax
import jax.numpy as jnp
from jax import lax
from jax.experimental import pallas as pl
from jax.experimental.pallas import tpu as pltpu

F32 = jnp.float32
BF16 = jnp.bfloat16
I32 = jnp.int32
HIGHEST = lax.Precision.HIGHEST

CHUNK = 64
EPS = 1e-6
CONV_WIDTH = 4
TOP_K = 4
N_MOD = 6
GLA_GATE_TEMP = 16.0
GLA_LOG_DECAY_FLOOR = -1.0
SWIGLU_LIMIT = 7.0
SWIGLU_ALPHA = 1.702
LANES = 128
SUBLANES = 8
NEG_BIG = -1e30
VMEM_LIMIT = 60 * 1024 * 1024


def _cparams(sem, row_gather=False):
    return pltpu.CompilerParams(dimension_semantics=sem, vmem_limit_bytes=VMEM_LIMIT,
                                disable_bounds_checks=row_gather)


def _tile(n, target, quantum):
    if n <= target:
        return n
    t = target - target % quantum
    while t > quantum and n % t:
        t -= quantum
    assert n % t == 0, (n, target, quantum)
    return t


def _dot(a, b, precision=None):
    return jnp.dot(a, b, preferred_element_type=F32, precision=precision)


def _dot_nt(a, b, precision=None):
    return lax.dot_general(a, b, (((1,), (1,)), ((), ())), preferred_element_type=F32,
                           precision=precision)


def _silu(x):
    return x * jax.nn.sigmoid(x)


def _softplus(x):
    return jnp.maximum(x, 0.0) + jnp.log1p(jnp.exp(-jnp.abs(x)))


def _rms(x, g):
    return x * lax.rsqrt(jnp.mean(x * x, axis=-1, keepdims=True) + EPS) * g


def _ada_kernel(c_ref, w_ref, b_ref, o_ref):
    a = _silu(c_ref[...]).astype(BF16)
    o_ref[...] = _dot(a, w_ref[...].astype(BF16)) + b_ref[...]


def _ada_proj(c, w, b):
    m, k = c.shape
    n = w.shape[1]
    tn = _tile(n, 512, LANES)
    return pl.pallas_call(
        _ada_kernel, grid=(n // tn,),
        in_specs=[pl.BlockSpec((m, k), lambda j: (0, 0)),
                  pl.BlockSpec((k, tn), lambda j: (0, j)),
                  pl.BlockSpec((1, tn), lambda j: (0, j))],
        out_specs=pl.BlockSpec((m, tn), lambda j: (0, j)),
        out_shape=jax.ShapeDtypeStruct((m, n), F32),
        compiler_params=_cparams(("arbitrary",)), name="ada_proj")(c, w, b.reshape(1, n))


def _mm_kernel(a_ref, w_ref, o_ref, w_bf):
    @pl.when(pl.program_id(1) == 0)
    def _():
        w_bf[...] = w_ref[...].astype(BF16)

    o_ref[...] = _dot(a_ref[...], w_bf[...]).astype(o_ref.dtype)


def _matmul(a, w, tm, tn, out_dtype, name):
    m, k = a.shape
    n = w.shape[1]
    return pl.pallas_call(
        _mm_kernel, grid=(n // tn, m // tm),
        in_specs=[pl.BlockSpec((tm, k), lambda j, i: (i, 0)),
                  pl.BlockSpec((k, tn), lambda j, i: (0, j))],
        out_specs=pl.BlockSpec((tm, tn), lambda j, i: (i, j)),
        out_shape=jax.ShapeDtypeStruct((m, n), out_dtype),
        scratch_shapes=[pltpu.VMEM((k, tn), BF16)],
        compiler_params=_cparams(("arbitrary", "arbitrary")), name=name)(a, w)


def _xp_spec(d, nch):
    return pl.BlockSpec((None, CHUNK, d), lambda s, b: (b, jnp.minimum(s, nch - 1), 0))


def _xs_spec(d, nch):
    return pl.BlockSpec((None, CHUNK, d), lambda s, b: (jnp.where(s < nch, 0, b), 0, 0))


def _mod_spec(d, nch):
    return pl.BlockSpec((None, None, N_MOD, d), lambda s, b: (jnp.where(s < nch, 0, 1), b, 0, 0))


def _row_spec(width, nb, col=0):
    return pl.BlockSpec((CHUNK, width), lambda s, b: (s * nb + b, col))


def _prenorm_kernel(xp_ref, xs_ref, mod_ref, g_ref, h_ref, *, nch):
    s = pl.program_id(0)
    x = jnp.where(s < nch, xp_ref[...], xs_ref[...])
    shift, scale = mod_ref[0:1, :], mod_ref[1:2, :]
    h_ref[...] = (_rms(x, g_ref[...]) * (1.0 + scale) + shift).astype(h_ref.dtype)


def _prenorm(xp, xs, mod, g, nch):
    nb, _, d = xp.shape
    t = (nch + 1) * nb * CHUNK
    return pl.pallas_call(
        functools.partial(_prenorm_kernel, nch=nch), grid=(nch + 1, nb),
        in_specs=[_xp_spec(d, nch), _xs_spec(d, nch), _mod_spec(d, nch),
                  pl.BlockSpec((1, d), lambda s, b: (0, 0))],
        out_specs=_row_spec(d, nb),
        out_shape=jax.ShapeDtypeStruct((t, d), BF16),
        compiler_params=_cparams(("arbitrary", "arbitrary")), name="prenorm")(xp, xs, mod, g)


def _to_token_tiles(dst_ref, rows):
    n, w = rows.shape
    nt = w // LANES
    for c in range(nt):
        dst_ref[pl.ds(c, n, stride=nt), :] = rows[:, c * LANES:(c + 1) * LANES]


def _slot_rows(nt):
    return nt + 1


def _from_token_tiles(src_ref, n, nt, c):
    return src_ref[pl.ds(c, n, stride=_slot_rows(nt)), :]


def _midnorm_kernel(xp_ref, xs_ref, y_ref, mod_ref, g1_ref, g2_ref, x1_ref, h2_ref, h2t_ref, *, nch):
    s = pl.program_id(0)
    x = jnp.where(s < nch, xp_ref[...], xs_ref[...])
    gate1, shift2, scale2 = mod_ref[2:3, :], mod_ref[3:4, :], mod_ref[4:5, :]
    x1 = x + gate1 * _rms(y_ref[...], g1_ref[...])
    x1_ref[...] = x1
    h2 = _rms(x1, g2_ref[...]) * (1.0 + scale2) + shift2
    h2_ref[...] = h2
    _to_token_tiles(h2t_ref, h2)


def _midnorm(xp, xs, y, mod, g1, g2, nch):
    nb, _, d = xp.shape
    t = (nch + 1) * nb * CHUNK
    vec = pl.BlockSpec((1, d), lambda s, b: (0, 0))
    nt = d // LANES
    tiles = pl.BlockSpec((CHUNK * nt, LANES), lambda s, b: (s * nb + b, 0))
    return pl.pallas_call(
        functools.partial(_midnorm_kernel, nch=nch), grid=(nch + 1, nb),
        in_specs=[_xp_spec(d, nch), _xs_spec(d, nch), _row_spec(d, nb), _mod_spec(d, nch), vec, vec],
        out_specs=[_row_spec(d, nb), _row_spec(d, nb), tiles],
        out_shape=[jax.ShapeDtypeStruct((t, d), F32), jax.ShapeDtypeStruct((t, d), F32),
                   jax.ShapeDtypeStruct((t * nt, LANES), F32)],
        compiler_params=_cparams(("arbitrary", "arbitrary")), name="midnorm")(xp, xs, y, mod, g1, g2)


def _prompt_state_spec(block, nch, nb):
    tail = (0,) * (len(block) - 1)
    return pl.BlockSpec(block, lambda s, b: (jnp.where(s == nch - 1, b, jnp.where(s < nch - 1, 0, nb - 1)),) + tail)


def _sample_state_spec(block, nch):
    tail = (0,) * (len(block) - 1)
    return pl.BlockSpec(block, lambda s, b: (jnp.where(s == nch, b, 0),) + tail)


def _tri(n, strict=False, dtype=F32):
    i = lax.broadcasted_iota(I32, (n, n), 0)
    j = lax.broadcasted_iota(I32, (n, n), 1)
    return ((i > j) if strict else (i >= j)).astype(dtype)


GDN_HEAD_GROUP = 16
assert 2 * CHUNK == LANES


def _gdn_kernel(qkv_ref, z_ref, ab_ref, convw_ref, conv0_ref, alog_ref, dtb_ref, gn_ref, s0_ref,
                oa_ref, convp_ref, convs_ref, gdnp_ref, gdns_ref,
                tails, state, full, act, qn, kn, *, nch, hq, hv, dh, group):
    s = pl.program_id(0)
    b = pl.program_id(1)
    kd = hq * dh
    rep = hv // hq

    @pl.when(s == 0)
    def _():
        tails[b] = jnp.zeros(tails.shape[1:], F32)
        state[b] = jnp.zeros(state.shape[1:], F32)

    @pl.when(s == nch)
    def _():
        tails[b] = conv0_ref[...]
        state[b] = s0_ref[...]

    full[0:SUBLANES, :] = tails[b]
    full[SUBLANES:SUBLANES + CHUNK, :] = qkv_ref[...]
    base = SUBLANES - (CONV_WIDTH - 1)
    acc = full[base:base + CHUNK, :] * convw_ref[0:1, :]
    for i in range(1, CONV_WIDTH):
        acc = acc + full[base + i:base + i + CHUNK, :] * convw_ref[i:i + 1, :]
    act[...] = _silu(acc)
    new_tail = full[CHUNK:CHUNK + SUBLANES, :]
    tails[b] = new_tail

    for h in range(hq):
        q = act[:, h * dh:(h + 1) * dh]
        k = act[:, kd + h * dh:kd + (h + 1) * dh]
        qn[:, h * dh:(h + 1) * dh] = q * lax.rsqrt(jnp.sum(q * q, axis=-1, keepdims=True) + EPS) * dh ** -0.5
        kn[:, h * dh:(h + 1) * dh] = k * lax.rsqrt(jnp.sum(k * k, axis=-1, keepdims=True) + EPS)

    ab = ab_ref[...]
    g_all = -jnp.exp(alog_ref[...]) * _softplus(ab + dtb_ref[...])
    beta_all = jax.nn.sigmoid(ab)
    gc_all = _dot(_tri(CHUNK), g_all, HIGHEST)
    gc_t = jnp.concatenate([gc_all, gc_all], axis=0).T
    eg_all = jnp.exp(gc_all)
    g_last = gc_all[CHUNK - 1:CHUNK, :]
    ekd_all = jnp.exp(g_last - gc_all)
    edec_all = jnp.exp(g_last)

    ii = lax.broadcasted_iota(I32, (CHUNK, LANES), 0)
    lane = lax.broadcasted_iota(I32, (CHUNK, LANES), 1)
    first = lane < CHUNK
    jj = jnp.where(first, lane, lane - CHUNK)
    incl = ii >= jj
    strict = ii > jj
    eye = (ii == jj).astype(F32)
    zeros_bf = jnp.zeros((CHUNK, LANES), BF16)

    def halves(x, with_rhs=True):
        hi = x.astype(BF16)
        rest = x - hi.astype(F32)
        rhs = None
        if with_rhs:
            rhs = jnp.concatenate([jnp.where(first, x, rest).astype(BF16), zeros_bf], axis=0)
        return hi, rest.astype(BF16), rhs

    def fold(r):
        sm = r[:CHUNK] + r[CHUNK:]
        return sm + pltpu.roll(sm, CHUNK, axis=1)

    for h0 in range(0, hv, group):
        heads = range(h0, min(h0 + group, hv))
        kh, qh, kb, kcat, dec, nn, tm = {}, {}, {}, {}, {}, {}, {}
        for h in heads:
            kh[h] = kn[:, (h // rep) * dh:(h // rep + 1) * dh]
            qh[h] = qn[:, (h // rep) * dh:(h // rep + 1) * dh]
            kcat[h] = jnp.concatenate([kh[h], kh[h]], axis=0).astype(BF16)
            beta = beta_all[:, hv + h:hv + h + 1]
            kb[h] = kh[h] * beta
            diff = gc_all[:, h:h + 1] - gc_t[h:h + 1, :]
            dec[h] = jnp.exp(jnp.where(incl, diff, NEG_BIG))
        for h in heads:
            a = _dot_nt(kb[h].astype(BF16), kcat[h])
            nn[h] = jnp.where(strict, -(a * dec[h]), 0.0)
            tm[h] = eye + nn[h]
        for h in heads:
            p_hi, p_lo, rhs = halves(nn[h])
            nn[h] = fold(_dot(jnp.concatenate([p_hi, p_lo], axis=0), rhs))
        p = 4
        while p < CHUNK:
            for h in heads:
                p_hi, p_lo, rhs = halves(nn[h])
                t_hi, t_lo, _ = halves(tm[h], with_rhs=False)
                r = _dot(jnp.concatenate([p_hi, p_lo, t_hi, t_lo], axis=0), rhs)
                nn[h] = fold(r[:2 * CHUNK])
                tm[h] = tm[h] + fold(r[2 * CHUNK:])
            p *= 2
        for h in heads:
            _, _, rhs = halves(nn[h])
            t_hi, t_lo, _ = halves(tm[h], with_rhs=False)
            tm[h] = tm[h] + fold(_dot(jnp.concatenate([t_hi, t_lo], axis=0), rhs))
        u, w, attn = {}, {}, {}
        for h in heads:
            beta = beta_all[:, hv + h:hv + h + 1]
            vh = act[:, 2 * kd + h * dh:2 * kd + (h + 1) * dh]
            rhs = jnp.concatenate([vh * beta, kb[h] * eg_all[:, h:h + 1]], axis=-1).astype(BF16)
            sol = _dot(tm[h][:, :CHUNK].astype(BF16), rhs)
            u[h], w[h] = sol[:, :dh], sol[:, dh:]
            attn[h] = (_dot_nt(qh[h].astype(BF16), kcat[h]) * dec[h])[:, :CHUNK].astype(BF16)
        vnew, o = {}, {}
        for h in heads:
            lhs = jnp.concatenate([w[h], qh[h] * eg_all[:, h:h + 1]], axis=0).astype(BF16)
            ws_qs = _dot(lhs, state[b, h].astype(BF16))
            vnew[h] = u[h] - ws_qs[:CHUNK]
            o[h] = ws_qs[CHUNK:] + _dot(attn[h], vnew[h].astype(BF16))
        for h in heads:
            kdec_t = (kh[h] * ekd_all[:, h:h + 1]).T.astype(BF16)
            state[b, h] = state[b, h] * edec_all[0:1, h:h + 1] + _dot(kdec_t, vnew[h].astype(BF16))
        for h in heads:
            zh = z_ref[:, h * dh:(h + 1) * dh]
            oa_ref[:, h * dh:(h + 1) * dh] = (_rms(o[h], gn_ref[...]) * _silu(zh)).astype(oa_ref.dtype)

    @pl.when(s == nch - 1)
    def _():
        convp_ref[...] = new_tail
        gdnp_ref[...] = state[b]

    @pl.when(s == nch)
    def _():
        convs_ref[...] = new_tail
        gdns_ref[...] = state[b]


def _gdn(p_main, p_small, conv_w, conv0, a_log, dt_bias, gdn_norm, s0, *, nch, nb, hq, hv, dh, z_col):
    t = p_main.shape[0]
    cd = conv_w.shape[1]
    vd = hv * dh
    kd = hq * dh
    vec = lambda n: pl.BlockSpec((1, n), lambda s, b: (0, 0))
    st_block = (None, hv, dh, dh)
    cv_block = (None, SUBLANES, cd)
    return pl.pallas_call(
        functools.partial(_gdn_kernel, nch=nch, hq=hq, hv=hv, dh=dh, group=GDN_HEAD_GROUP), grid=(nch + 1, nb),
        in_specs=[_row_spec(cd, nb), _row_spec(vd, nb, z_col), _row_spec(LANES, nb),
                  pl.BlockSpec((CONV_WIDTH, cd), lambda s, b: (0, 0)),
                  _sample_state_spec(cv_block, nch), vec(LANES), vec(LANES), vec(dh),
                  _sample_state_spec(st_block, nch)],
        out_specs=[_row_spec(vd, nb),
                   _prompt_state_spec(cv_block, nch, nb), _sample_state_spec(cv_block, nch),
                   _prompt_state_spec(st_block, nch, nb), _sample_state_spec(st_block, nch)],
        out_shape=[jax.ShapeDtypeStruct((t, vd), BF16),
                   jax.ShapeDtypeStruct((nb, SUBLANES, cd), F32), jax.ShapeDtypeStruct((nb, SUBLANES, cd), F32),
                   jax.ShapeDtypeStruct((nb, hv, dh, dh), F32), jax.ShapeDtypeStruct((nb, hv, dh, dh), F32)],
        scratch_shapes=[pltpu.VMEM((nb, SUBLANES, cd), F32), pltpu.VMEM((nb, hv, dh, dh), F32),
                        pltpu.VMEM((SUBLANES + CHUNK, cd), F32), pltpu.VMEM((CHUNK, cd), F32),
                        pltpu.VMEM((CHUNK, kd), F32), pltpu.VMEM((CHUNK, kd), F32)],
        compiler_params=_cparams(("arbitrary", "arbitrary")), name="gdn")(
            p_main, p_main, p_small, conv_w, conv0, a_log, dt_bias, gdn_norm, s0)


def _gla_kernel(q_ref, k_ref, v_ref, r_ref, ab_ref, w2_ref, gb_ref, gn_ref, s0_ref,
                ob_ref, glap_ref, glas_ref, state, *, nch, nh, hk, hvd):
    s = pl.program_id(0)
    b = pl.program_id(1)

    @pl.when(s == 0)
    def _():
        state[b] = jnp.zeros(state.shape[1:], F32)

    @pl.when(s == nch)
    def _():
        state[b] = s0_ref[...]

    pre = _dot(ab_ref[...], w2_ref[...], HIGHEST) + gb_ref[...]
    log_sig = jnp.minimum(pre, 0.0) - jnp.log1p(jnp.exp(-jnp.abs(pre)))
    gk = jnp.maximum(log_sig / GLA_GATE_TEMP, GLA_LOG_DECAY_FLOOR)
    bc = _dot(_tri(CHUNK), gk, HIGHEST)
    b_last = bc[CHUNK - 1:CHUNK, :]
    qd = (q_ref[...] * hk ** -0.5 * jnp.exp(bc)).astype(BF16)
    ki = (k_ref[...] * jnp.exp(-bc)).astype(BF16)
    kdec = k_ref[...] * jnp.exp(b_last - bc)
    edec = jnp.exp(b_last)
    incl = _tri(CHUNK) > 0.0

    for h in range(nh):
        ks = slice(h * hk, (h + 1) * hk)
        vs = slice(h * hvd, (h + 1) * hvd)
        vh = v_ref[:, vs].astype(BF16)
        attn = jnp.where(incl, _dot_nt(qd[:, ks], ki[:, ks]), 0.0).astype(BF16)
        o = _dot(attn, vh) + _dot(qd[:, ks], state[b, h].astype(BF16))
        edec_col = jnp.broadcast_to(edec[:, ks], (SUBLANES, hk)).T[:, 0:1]
        state[b, h] = state[b, h] * edec_col + _dot(kdec[:, ks].T.astype(BF16), vh)
        ob_ref[:, vs] = (_rms(o, gn_ref[...]) * _silu(r_ref[:, vs])).astype(ob_ref.dtype)

    @pl.when(s == nch - 1)
    def _():
        glap_ref[...] = state[b]

    @pl.when(s == nch)
    def _():
        glas_ref[...] = state[b]


def _gla(p_main, p_small, w2_pad, gate_b, gla_norm, s0, *, nch, nb, nh, hk, hvd, cols):
    t = p_main.shape[0]
    kl, vl = nh * hk, nh * hvd
    st_block = (None, nh, hk, hvd)
    q_col, k_col, v_col, r_col = cols
    return pl.pallas_call(
        functools.partial(_gla_kernel, nch=nch, nh=nh, hk=hk, hvd=hvd), grid=(nch + 1, nb),
        in_specs=[_row_spec(kl, nb, q_col), _row_spec(kl, nb, k_col), _row_spec(vl, nb, v_col),
                  _row_spec(vl, nb, r_col), _row_spec(LANES, nb),
                  pl.BlockSpec((LANES, kl), lambda s, b: (0, 0)),
                  pl.BlockSpec((1, kl), lambda s, b: (0, 0)),
                  pl.BlockSpec((1, hvd), lambda s, b: (0, 0)),
                  _sample_state_spec(st_block, nch)],
        out_specs=[_row_spec(vl, nb), _prompt_state_spec(st_block, nch, nb), _sample_state_spec(st_block, nch)],
        out_shape=[jax.ShapeDtypeStruct((t, vl), BF16),
                   jax.ShapeDtypeStruct((nb, nh, hk, hvd), F32), jax.ShapeDtypeStruct((nb, nh, hk, hvd), F32)],
        scratch_shapes=[pltpu.VMEM((nb, nh, hk, hvd), F32)],
        compiler_params=_cparams(("arbitrary", "arbitrary")), name="gla")(
            p_main, p_main, p_main, p_main, p_small, w2_pad, gate_b, gla_norm, s0)


def _merge_kernel(oa_ref, ob_ref, wa_ref, wb_ref, g0_ref, g1_ref, o_ref):
    ya = _dot(oa_ref[...], wa_ref[...])
    yb = _dot(ob_ref[...], wb_ref[...])
    o_ref[...] = (jax.nn.sigmoid(g0_ref[...]) * ya + jax.nn.sigmoid(g1_ref[...]) * yb).astype(o_ref.dtype)


def _merge(oa, ob, wa, wb, p_main, br_off, tm, tn):
    t, ka = oa.shape
    kb = ob.shape[1]
    d = wa.shape[1]
    c0, c1 = br_off // tn, (br_off + d) // tn
    return pl.pallas_call(
        _merge_kernel, grid=(d // tn, t // tm),
        in_specs=[pl.BlockSpec((tm, ka), lambda j, i: (i, 0)), pl.BlockSpec((tm, kb), lambda j, i: (i, 0)),
                  pl.BlockSpec((ka, tn), lambda j, i: (0, j)), pl.BlockSpec((kb, tn), lambda j, i: (0, j)),
                  pl.BlockSpec((tm, tn), lambda j, i: (i, c0 + j)), pl.BlockSpec((tm, tn), lambda j, i: (i, c1 + j))],
        out_specs=pl.BlockSpec((tm, tn), lambda j, i: (i, j)),
        out_shape=jax.ShapeDtypeStruct((t, d), BF16),
        compiler_params=_cparams(("arbitrary", "arbitrary")), name="merge")(oa, ob, wa, wb, p_main, p_main)


def _router_kernel(h_ref, w_ref, b_ref, idx_ref, wgt_ref, rank_ref, cnt_ref, run, *, tm):
    @pl.when(pl.program_id(0) == 0)
    def _():
        run[...] = jnp.zeros(run.shape, F32)

    logits = _dot(h_ref[...], w_ref[...], HIGHEST) + b_ref[...]
    lane = lax.broadcasted_iota(I32, logits.shape, 1).astype(F32)
    cur = logits
    tops, sels, idxs = [], [], []
    for _ in range(TOP_K):
        m = jnp.max(cur, axis=-1, keepdims=True)
        idx = jnp.min(jnp.where(cur == m, lane, float(LANES)), axis=-1, keepdims=True)
        sel = lane == idx
        tops.append(m)
        sels.append(sel)
        idxs.append(idx)
        cur = jnp.where(sel, 2.0 * NEG_BIG, cur)
    exps = [jnp.exp(m - tops[0]) for m in tops]
    denom = exps[0]
    for e in exps[1:]:
        denom = denom + e
    onehot = sels[0].astype(F32)
    for sel in sels[1:]:
        onehot = onehot + sel.astype(F32)
    prefix = _dot(_tri(tm, strict=True, dtype=BF16), onehot.astype(BF16)) + run[0:1, :]
    idx_out = jnp.zeros(logits.shape, F32)
    wgt_out = jnp.zeros(logits.shape, F32)
    rank_out = jnp.zeros(logits.shape, F32)
    for k in range(TOP_K):
        rank = jnp.sum(jnp.where(sels[k], prefix, 0.0), axis=-1, keepdims=True)
        idx_out = jnp.where(lane == float(k), idxs[k], idx_out)
        wgt_out = jnp.where(lane == float(k), exps[k] / denom, wgt_out)
        rank_out = jnp.where(lane == float(k), rank, rank_out)
    idx_ref[...] = idx_out.astype(I32)
    wgt_ref[...] = wgt_out
    rank_ref[...] = rank_out.astype(I32)
    total = run[...] + jnp.sum(onehot, axis=0, keepdims=True)
    run[...] = total
    cnt_ref[...] = total.astype(I32)


def _router(h2, w_pad, b_pad, tm):
    t, d = h2.shape
    row = pl.BlockSpec((tm, LANES), lambda i: (i, 0))
    return pl.pallas_call(
        functools.partial(_router_kernel, tm=tm), grid=(t // tm,),
        in_specs=[pl.BlockSpec((tm, d), lambda i: (i, 0)), pl.BlockSpec((d, LANES), lambda i: (0, 0)),
                  pl.BlockSpec((1, LANES), lambda i: (0, 0))],
        out_specs=[row, row, row, pl.BlockSpec((SUBLANES, LANES), lambda i: (0, 0))],
        out_shape=[jax.ShapeDtypeStruct((t, LANES), I32), jax.ShapeDtypeStruct((t, LANES), F32),
                   jax.ShapeDtypeStruct((t, LANES), I32), jax.ShapeDtypeStruct((SUBLANES, LANES), I32)],
        scratch_shapes=[pltpu.VMEM((SUBLANES, LANES), F32)],
        compiler_params=_cparams(("arbitrary",)), name="router")(h2, w_pad, b_pad)


def _row_copy(src_hbm, dst, row, slot, nt, sem):
    return pltpu.make_async_copy(src_hbm.at[row], dst.at[pl.ds(slot * _slot_rows(nt), nt)], sem)


GATHER_UNROLL = 8


def _dispatch_kernel(nu_ref, tok_ref, tok_next_ref, src_hbm, o_ref, buf, sems, *, tg, nt):
    i = pl.program_id(0)
    slot = i % 2

    def start_block(toks, s):
        def issue(r, c):
            _row_copy(src_hbm, buf.at[s], toks[r], r, nt, sems.at[s]).start()
            return c

        lax.fori_loop(0, tg, issue, 0, unroll=GATHER_UNROLL)

    @pl.when(jnp.logical_and(i == 0, i < nu_ref[0]))
    def _():
        start_block(tok_ref, slot)

    @pl.when(i + 1 < nu_ref[0])
    def _():
        start_block(tok_next_ref, 1 - slot)

    @pl.when(i < nu_ref[0])
    def _():
        def drain(r, c):
            _row_copy(src_hbm, buf.at[slot], tok_ref[r], r, nt, sems.at[slot]).wait()
            return c

        lax.fori_loop(0, tg, drain, 0, unroll=GATHER_UNROLL)
        for c in range(nt):
            o_ref[:, c * LANES:(c + 1) * LANES] = _from_token_tiles(buf.at[slot], tg, nt, c).astype(o_ref.dtype)

    @pl.when(i >= nu_ref[0])
    def _():
        o_ref[...] = jnp.zeros(o_ref.shape, o_ref.dtype)


def _dispatch(row_token, n_used, h2_tiles, tg):
    r = row_token.shape[0]
    nt = h2_tiles.shape[1]
    d = nt * LANES
    return pl.pallas_call(
        functools.partial(_dispatch_kernel, tg=tg, nt=nt),
        grid_spec=pltpu.PrefetchScalarGridSpec(
            num_scalar_prefetch=1, grid=(r // tg,),
            in_specs=[pl.BlockSpec((tg,), lambda i, nu: (i,), memory_space=pltpu.SMEM),
                      pl.BlockSpec((tg,), lambda i, nu: (jnp.minimum(i + 1, r // tg - 1),),
                                   memory_space=pltpu.SMEM),
                      pl.BlockSpec(memory_space=pl.ANY)],
            out_specs=pl.BlockSpec((tg, d), lambda i, nu: (i, 0)),
            scratch_shapes=[pltpu.VMEM((2, tg * _slot_rows(nt), LANES), F32), pltpu.SemaphoreType.DMA((2,))]),
        out_shape=jax.ShapeDtypeStruct((r, d), BF16),
        compiler_params=_cparams(("arbitrary",), row_gather=True), name="moe_dispatch")(
            n_used, row_token, row_token, h2_tiles)


META_FIELDS = 4


def _weight_copy(w_hbm, land, sems, slot, part, expert, col, tn):
    src = w_hbm.at[expert, :, pl.ds(pl.multiple_of(col, tn), tn)]
    return pltpu.make_async_copy(src, land.at[slot, part], sems.at[slot, part])


def _swap_in_weights(be_ref, nu_ref, meta_ref, w_hbm, land, sems, dst_bf, col_of):
    j, i = pl.program_id(0), pl.program_id(1)
    n_sweeps = pl.num_programs(0)
    tn = dst_bf[0].shape[1]
    parts = range(len(dst_bf))

    @pl.when(jnp.logical_and(i < nu_ref[0], meta_ref[META_FIELDS * i] == 1))
    def _():
        run = j * nu_ref[1] + meta_ref[META_FIELDS * i + 1]
        slot = run % 2

        @pl.when(run == 0)
        def _():
            for p in parts:
                _weight_copy(w_hbm, land, sems, slot, p, be_ref[i], col_of(p, j), tn).start()

        for p in parts:
            _weight_copy(w_hbm, land, sems, slot, p, be_ref[i], col_of(p, j), tn).wait()
            dst_bf[p][...] = land[slot, p].astype(BF16)

        next_sweep = j + meta_ref[META_FIELDS * i + 2]

        @pl.when(next_sweep < n_sweeps)
        def _():
            for p in parts:
                _weight_copy(w_hbm, land, sems, 1 - slot, p, meta_ref[META_FIELDS * i + 3],
                             col_of(p, next_sweep), tn).start()


def _for_filled_rows(i, nu_ref, half_ref, o_ref, compute):
    tm = o_ref.shape[0]
    used = i < nu_ref[0]
    half_full = half_ref[i] == 1

    @pl.when(jnp.logical_and(used, jnp.logical_not(half_full)))
    def _():
        compute(slice(0, tm))

    @pl.when(jnp.logical_and(used, half_full))
    def _():
        compute(slice(0, tm // 2))
        o_ref[tm // 2:] = jnp.zeros((tm - tm // 2,) + o_ref.shape[1:], o_ref.dtype)

    @pl.when(jnp.logical_not(used))
    def _():
        o_ref[...] = jnp.zeros(o_ref.shape, o_ref.dtype)


def _gate_up_kernel(be_ref, nu_ref, half_ref, meta_ref, x_ref, w_hbm, bg_ref, bl_ref, o_ref,
                    land, wg_bf, wl_bf, sems):
    i = pl.program_id(1)
    tn = wg_bf.shape[1]
    n_sweeps = pl.num_programs(0)
    _swap_in_weights(be_ref, nu_ref, meta_ref, w_hbm, land, sems, (wg_bf, wl_bf),
                     lambda part, sweep: (part * n_sweeps + sweep) * tn)

    def compute(rows):
        x = x_ref[rows, :]
        glu = jnp.minimum(_dot(x, wg_bf[...]) + bg_ref[...], SWIGLU_LIMIT)
        lin = jnp.clip(_dot(x, wl_bf[...]) + bl_ref[...], -SWIGLU_LIMIT, SWIGLU_LIMIT)
        o_ref[rows, :] = (glu * jax.nn.sigmoid(SWIGLU_ALPHA * glu) * (lin + 1.0)).astype(o_ref.dtype)

    _for_filled_rows(i, nu_ref, half_ref, o_ref, compute)


def _down_kernel(be_ref, nu_ref, half_ref, meta_ref, a_ref, w_hbm, b_ref, o_ref, land, w_bf, sems):
    i = pl.program_id(1)
    tn = w_bf.shape[1]
    _swap_in_weights(be_ref, nu_ref, meta_ref, w_hbm, land, sems, (w_bf,), lambda part, sweep: sweep * tn)

    def compute(rows):
        o_ref[rows, :] = _dot(a_ref[rows, :], w_bf[...]) + b_ref[...]

    _for_filled_rows(i, nu_ref, half_ref, o_ref, compute)


def _moe_experts(xs, block_expert, n_used, block_half, meta, w_gate_up, b_gate_up, w_down, b_down, tm):
    r, d = xs.shape
    ne, _, two_de = w_gate_up.shape
    de = two_de // 2
    nb = r // tm
    tn = _tile(de, 512, LANES)
    nj = de // tn

    def blk(i, nu):
        return jnp.maximum(jnp.minimum(i, nu[0] - 1), 0)

    hbm = pl.BlockSpec(memory_space=pl.ANY)
    act = pl.pallas_call(
        _gate_up_kernel,
        grid_spec=pltpu.PrefetchScalarGridSpec(
            num_scalar_prefetch=4, grid=(nj, nb),
            in_specs=[pl.BlockSpec((tm, d), lambda j, i, be, nu, hf, mt: (blk(i, nu), 0)), hbm,
                      pl.BlockSpec((None, 1, tn), lambda j, i, be, nu, hf, mt: (be[blk(i, nu)], 0, j)),
                      pl.BlockSpec((None, 1, tn), lambda j, i, be, nu, hf, mt: (be[blk(i, nu)], 0, nj + j))],
            out_specs=pl.BlockSpec((tm, tn), lambda j, i, be, nu, hf, mt: (i, j)),
            scratch_shapes=[pltpu.VMEM((2, 2, d, tn), F32), pltpu.VMEM((d, tn), BF16), pltpu.VMEM((d, tn), BF16),
                            pltpu.SemaphoreType.DMA((2, 2))]),
        out_shape=jax.ShapeDtypeStruct((r, de), BF16),
        compiler_params=_cparams(("arbitrary", "arbitrary")), name="moe_gate_up")(
            block_expert, n_used, block_half, meta, xs, w_gate_up,
            b_gate_up.reshape(ne, 1, two_de), b_gate_up.reshape(ne, 1, two_de))

    dm = w_down.shape[2]
    tn = _tile(dm, 1024, LANES)
    njd = dm // tn
    return pl.pallas_call(
        _down_kernel,
        grid_spec=pltpu.PrefetchScalarGridSpec(
            num_scalar_prefetch=4, grid=(njd, nb),
            in_specs=[pl.BlockSpec((tm, de), lambda j, i, be, nu, hf, mt: (blk(i, nu), 0)), hbm,
                      pl.BlockSpec((None, 1, tn), lambda j, i, be, nu, hf, mt: (be[blk(i, nu)], 0, j))],
            out_specs=pl.BlockSpec((tm, tn), lambda j, i, be, nu, hf, mt: (i, j)),
            scratch_shapes=[pltpu.VMEM((2, 1, de, tn), F32), pltpu.VMEM((de, tn), BF16),
                            pltpu.SemaphoreType.DMA((2, 1))]),
        out_shape=jax.ShapeDtypeStruct((r, dm), F32),
        compiler_params=_cparams(("arbitrary", "arbitrary")), name="moe_down")(
            block_expert, n_used, block_half, meta, act, w_down, b_down.reshape(ne, 1, dm))


def _strided_row_copy(src_hbm, dst, row, slot, sem):
    return pltpu.make_async_copy(src_hbm.at[pl.ds(row, 1)], dst.at[pl.ds(slot, 1)], sem)


def _combine_kernel(dest_ref, wgt_ref, x1_ref, mod_ref, g_ref, yb_hbm, op_ref, os_ref, buf, sem, *, nch):
    s = pl.program_id(0)

    def issue(n, c):
        t, k = n // TOP_K, n % TOP_K
        _strided_row_copy(yb_hbm, buf.at[k], dest_ref[n], t, sem).start()
        return c

    def drain(n, c):
        t, k = n // TOP_K, n % TOP_K
        _strided_row_copy(yb_hbm, buf.at[k], dest_ref[n], t, sem).wait()
        return c

    lax.fori_loop(0, CHUNK * TOP_K, issue, 0, unroll=GATHER_UNROLL)
    lax.fori_loop(0, CHUNK * TOP_K, drain, 0, unroll=GATHER_UNROLL)
    wgt = wgt_ref[...]
    y = wgt[:, 0:1] * buf[0]
    for k in range(1, TOP_K):
        y = y + wgt[:, k:k + 1] * buf[k]
    out = x1_ref[...] + mod_ref[5:6, :] * _rms(y, g_ref[...])

    @pl.when(s < nch)
    def _():
        op_ref[...] = out

    @pl.when(s == nch)
    def _():
        os_ref[...] = out


def _combine(dest, wgt, x1, mod, g, yb, nch, nb, seq):
    t, d = x1.shape
    return pl.pallas_call(
        functools.partial(_combine_kernel, nch=nch), grid=(nch + 1, nb),
        in_specs=[pl.BlockSpec((CHUNK * TOP_K,), lambda s, b: (s * nb + b,), memory_space=pltpu.SMEM),
                  _row_spec(LANES, nb), _row_spec(d, nb), _mod_spec(d, nch),
                  pl.BlockSpec((1, d), lambda s, b: (0, 0)),
                  pl.BlockSpec(memory_space=pl.ANY)],
        out_specs=[pl.BlockSpec((None, CHUNK, d),
                                lambda s, b: (jnp.where(s < nch, b, nb - 1), jnp.minimum(s, nch - 1), 0)),
                   pl.BlockSpec((None, CHUNK, d), lambda s, b: (jnp.where(s == nch, b, 0), 0, 0))],
        out_shape=[jax.ShapeDtypeStruct((nb, seq, d), F32), jax.ShapeDtypeStruct((nb, CHUNK, d), F32)],
        scratch_shapes=[pltpu.VMEM((TOP_K, CHUNK, d), F32), pltpu.SemaphoreType.DMA(())],
        compiler_params=_cparams(("arbitrary", "arbitrary"), row_gather=True), name="moe_combine")(
            dest, wgt, x1, mod, g, yb)


def _layer(x_prompt, x_sample, c_prompt, c_sample, conv_s0, gdn_s0, gla_s0,
           w_ada, b_ada, n_mix_pre, n_mix_post, n_ffn_pre, n_ffn_post, w_in, conv_w, a_log, dt_bias,
           gdn_norm, gla_w2, gla_b, gla_norm, w_branch, w_out, w_router, b_router,
           w_gate_up, b_gate_up, w_down, b_down):
    nb, seq, d = x_prompt.shape
    assert x_sample.shape == (nb, CHUNK, d) and seq % CHUNK == 0
    nch = seq // CHUNK
    t = (nch + 1) * nb * CHUNK
    hv, dh = a_log.shape[0], gdn_norm.shape[0]
    cd = conv_w.shape[1]
    vd = hv * dh
    kd = (cd - vd) // 2
    hq = kd // dh
    nh, hk, hvd = gla_s0.shape[1:]
    kl, vl = nh * hk, nh * hvd
    rank = gla_w2.shape[0]
    ne = w_router.shape[1]
    assert 2 * hv + rank <= LANES and ne <= LANES

    c_all = jnp.concatenate([c_prompt, c_sample], axis=0)
    mod = _ada_proj(c_all, w_ada, b_ada).reshape(2, nb, N_MOD, d)

    sizes = (cd, hv, hv, vd, kl, kl, vl, rank, vl, 2 * d)
    offs = [0]
    for sz in sizes:
        offs.append(offs[-1] + sz)
    seg = lambda i: w_in[:, offs[i]:offs[i + 1]]
    main_ids = (0, 3, 4, 5, 6, 8, 9)
    w_main = jnp.concatenate([seg(i) for i in main_ids], axis=1)
    n_small = 2 * hv + rank
    w_small = jnp.concatenate([seg(1), seg(2), seg(7), jnp.zeros((d, LANES - n_small), F32)], axis=1)
    moff = [0]
    for i in main_ids:
        moff.append(moff[-1] + sizes[i])
    z_off, lq_off, lk_off, lv_off, lr_off, br_off = moff[1:7]
    assert z_off % vd == 0 and lq_off % kl == 0 and lk_off % kl == 0 and lv_off % vl == 0 and lr_off % vl == 0

    h1 = _prenorm(x_prompt, x_sample, mod, n_mix_pre.reshape(1, d), nch)
    tm = _tile(t, 512, CHUNK)
    p_main = _matmul(h1, w_main, tm, _tile(w_main.shape[1], 1024, LANES), F32, "in_proj")
    p_small = _matmul(h1, w_small, tm, LANES, F32, "in_proj_small")

    pad_lanes = lambda v: jnp.pad(v.reshape(1, -1), ((0, 0), (0, LANES - v.shape[0])))
    conv0 = jnp.pad(conv_s0, ((0, 0), (SUBLANES - (CONV_WIDTH - 1), 0), (0, 0)))
    oa, convp, convs, gdnp, gdns = _gdn(
        p_main, p_small, conv_w, conv0, pad_lanes(a_log), pad_lanes(dt_bias), gdn_norm.reshape(1, dh), gdn_s0,
        nch=nch, nb=nb, hq=hq, hv=hv, dh=dh, z_col=z_off // vd)
    w2_pad = jnp.zeros((LANES, kl), F32).at[2 * hv:2 * hv + rank].set(gla_w2)
    ob, glap, glas = _gla(
        p_main, p_small, w2_pad, gla_b.reshape(1, kl), gla_norm.reshape(1, hvd), gla_s0,
        nch=nch, nb=nb, nh=nh, hk=hk, hvd=hvd,
        cols=(lq_off // kl, lk_off // kl, lv_off // vl, lr_off // vl))

    tn = _tile(d, 1024, LANES)
    assert br_off % tn == 0
    ym = _merge(oa, ob, w_branch[:vd].astype(BF16), w_branch[vd:].astype(BF16), p_main, br_off, tm, tn)
    y2 = _matmul(ym, w_out, tm, tn, F32, "out_proj")
    x1, h2, h2_tiles = _midnorm(x_prompt, x_sample, y2, mod, n_mix_post.reshape(1, d), n_ffn_pre.reshape(1, d), nch)

    w_r = jnp.pad(w_router, ((0, 0), (0, LANES - ne)))
    b_r = jnp.pad(b_router.reshape(1, ne), ((0, 0), (0, LANES - ne)), constant_values=NEG_BIG)
    idx, wgt, rnk, cnt = _router(h2, w_r, b_r, tm)
    tmm = _tile(t * TOP_K, 512, CHUNK)
    counts = cnt[0, :ne]
    padded = (counts + tmm - 1) // tmm * tmm
    pad_end = jnp.cumsum(padded)
    pad_start = pad_end - padded
    dest = (pad_start[idx[:, :TOP_K]] + rnk[:, :TOP_K]).astype(I32)
    n_blocks = t * TOP_K // tmm + ne
    n_rows = n_blocks * tmm
    token_of_pair = jnp.repeat(jnp.arange(t, dtype=I32), TOP_K)
    row_token = jnp.zeros((n_rows,), I32).at[dest.reshape(-1)].set(token_of_pair)
    block_start = jnp.arange(n_blocks, dtype=I32) * tmm
    block_expert = jnp.minimum(jnp.sum(block_start[:, None] >= pad_end[None, :], axis=1), ne - 1).astype(I32)
    filled = counts[block_expert] - (block_start - pad_start[block_expert])
    block_half = (filled <= tmm // 2).astype(I32)
    present = counts > 0
    ids = jnp.arange(ne, dtype=I32)
    later = jnp.logical_and(ids[None, :] > ids[:, None], present[None, :])
    has_later = jnp.any(later, axis=1)
    next_expert = jnp.where(has_later, jnp.argmax(later, axis=1), jnp.argmax(present)).astype(I32)
    run_of_expert = (jnp.cumsum(present) - 1).astype(I32)
    n_used_blocks = pad_end[-1] // tmm
    n_used = jnp.stack([n_used_blocks, jnp.sum(present)]).astype(I32)
    prev_expert = jnp.concatenate([jnp.full((1,), -1, I32), block_expert[:-1]])
    first = jnp.logical_and(block_expert != prev_expert, block_start < pad_end[-1])
    meta = jnp.stack([first.astype(I32), run_of_expert[block_expert],
                      jnp.logical_not(has_later)[block_expert].astype(I32), next_expert[block_expert]],
                     axis=1).reshape(-1)
    assert meta.shape[0] == META_FIELDS * n_blocks

    xs = _dispatch(row_token, n_used, h2_tiles.reshape(t, d // LANES, LANES), tmm)
    yb = _moe_experts(xs, block_expert, n_used, block_half, meta, w_gate_up, b_gate_up, w_down, b_down, tmm)
    y_prompt, y_sample = _combine(dest.reshape(-1), wgt, x1, mod, n_ffn_post.reshape(1, d), yb, nch, nb, seq)
    tail = slice(SUBLANES - (CONV_WIDTH - 1), SUBLANES)
    return y_prompt, y_sample, convp[:, tail], gdnp, glap, convs[:, tail], gdns, glas


def kernel(x_prompt, x_sample, c_prompt, c_sample, state_conv, state_gdn, state_gla, w_ada, b_ada, norm_mix_pre, norm_mix_post, norm_ffn_pre, norm_ffn_post, w_in, conv_w, gdn_a_log, gdn_dt_bias, gdn_norm, gla_gate_w2, gla_gate_b, gla_norm, w_branch, w_out, w_router, b_router, w_gate_up, b_gate_up, w_down, b_down):
    assert w_ada.shape[0] == 1, "single-layer step"
    weights = (w_ada, b_ada, norm_mix_pre, norm_mix_post, norm_ffn_pre, norm_ffn_post, w_in, conv_w,
               gdn_a_log, gdn_dt_bias, gdn_norm, gla_gate_w2, gla_gate_b, gla_norm, w_branch, w_out,
               w_router, b_router, w_gate_up, b_gate_up, w_down, b_down)
    yp, ys, cp, gp, lp, cs, gs, ls = _layer(
        x_prompt, x_sample, c_prompt, c_sample, state_conv[0], state_gdn[0], state_gla[0],
        *(w[0] for w in weights))
    return yp, ys, cp[None], gp[None], lp[None], cs[None], gs[None], ls[None]
```
